```python
import math
import jax
import jax.numpy as jnp
from jax import lax
import numpy as np


D_MODEL = 4096
BATCH = 2
SEQ = 8192
DEPTH = 2

CTX_LEN = 256
GRID_W = 64

RET_WIDTH = D_MODEL // 4
RET_HEAD_DIM = 128
RET_HEADS = RET_WIDTH // RET_HEAD_DIM
SSD_WIDTH = D_MODEL // 2
SSD_HEAD_DIM = 64
SSD_HEADS = SSD_WIDTH // SSD_HEAD_DIM
SSD_GROUPS = 8
SSD_HEADS_PER_GROUP = SSD_HEADS // SSD_GROUPS
SSD_STATE = 128
SSD_CONV = 5
SSD_CONV_DIM = SSD_WIDTH + 2 * SSD_GROUPS * SSD_STATE
DIFF_WIDTH = D_MODEL // 4
DIFF_V_DIM = 128
DIFF_HEADS = DIFF_WIDTH // DIFF_V_DIM
DIFF_QK_DIM = DIFF_V_DIM // 2
MIX_WIDTH = RET_WIDTH + SSD_WIDTH + DIFF_WIDTH

IN_SIZES = (RET_WIDTH, RET_WIDTH, RET_WIDTH, RET_WIDTH,
            SSD_WIDTH, SSD_CONV_DIM, 2 * SSD_HEADS,
            2 * DIFF_HEADS * DIFF_QK_DIM, 2 * DIFF_HEADS * DIFF_QK_DIM, DIFF_WIDTH)
IN_WIDTH = sum(IN_SIZES)
IN_SPLITS = tuple(int(s) for s in np.cumsum(IN_SIZES)[:-1])

SCAN_CHUNK = 128
Q_BLOCK = 128
ROPE_BASE = 10000.0
D_FF = 11008
N_EXPERTS = 8
TOP_K = 2
D_FF_EXPERT = D_MODEL
MOE_BLOCK = 256
NORM_EPS = 1e-6
ADA_SCALE = 0.5

kernel_name = 'hybrid_retention_ssd_diffattn_moe_dit'


def rmsnorm(x, w):
    xf = x.astype(jnp.float32)
    xf = xf * lax.rsqrt(jnp.mean(xf * xf, axis=-1, keepdims=True) + NORM_EPS)
    return xf.astype(x.dtype) * w


def modulate(h, shift, scale):
    return h * (1 + scale) + shift


def rope_2d_tables(seq, head_dim):
    rows = seq // GRID_W
    r = jnp.broadcast_to(jnp.arange(rows, dtype=jnp.float32)[:, None], (rows, GRID_W)).reshape(seq)
    col = jnp.broadcast_to(jnp.arange(GRID_W, dtype=jnp.float32)[None, :], (rows, GRID_W)).reshape(seq)
    n_freq = head_dim // 4
    inv_freq = ROPE_BASE ** (-jnp.arange(n_freq, dtype=jnp.float32) / n_freq)
    ang_r = r[:, None] * inv_freq
    ang_c = col[:, None] * inv_freq
    ang = jnp.concatenate([ang_r, ang_r, ang_c, ang_c], axis=-1)
    return jnp.cos(ang), jnp.sin(ang)


def apply_rope_2d(t, cos, sin):
    a1, a2, b1, b2 = jnp.split(t, 4, axis=-1)
    rot = jnp.concatenate([-a2, a1, -b2, b1], axis=-1)
    return t * cos[:, None, :].astype(t.dtype) + rot * sin[:, None, :].astype(t.dtype)


def chunked_linear_scan(q, k, v, log_a, s0):
    bsz, seq, g, n = q.shape
    r, p = v.shape[-2:]
    nc = seq // SCAN_CHUNK

    def chunks(t):
        t = t.astype(jnp.float32)
        return jnp.moveaxis(t.reshape((bsz, nc, SCAN_CHUNK) + t.shape[2:]), 1, 0)

    lower = jnp.tril(jnp.ones((SCAN_CHUNK, SCAN_CHUNK), dtype=bool))[None, :, :, None, None]

    def step(s, inp):
        qc, kc, vc, ac = inp
        cs = jnp.cumsum(ac, axis=1)
        seg = cs[:, :, None] - cs[:, None, :]
        decay = jnp.exp(jnp.where(lower, seg, -jnp.inf))
        scores = jnp.einsum('bign,bjgn->bijg', qc, kc)
        y = jnp.einsum('bijg,bijgr,bjgrp->bigrp', scores, decay, vc)
        y = y + jnp.einsum('bign,bgrnp->bigrp', qc, s) * jnp.exp(cs)[..., None]
        tail = jnp.exp(cs[:, -1:] - cs)
        s = jnp.exp(cs[:, -1])[..., None, None] * s + jnp.einsum('bjgn,bjgr,bjgrp->bgrnp', kc, tail, vc)
        return s, y

    s, y = lax.scan(step, s0.astype(jnp.float32), (chunks(q), chunks(k), chunks(v), chunks(log_a)))
    y = jnp.moveaxis(y, 0, 1).reshape(bsz, seq, g, r, p)
    return y, s


def bidir_scan(q, k, v_fb, a_fb, qc, kc, vc_fb, ac_fb):
    flip = lambda t: jnp.flip(t, axis=1)
    bsz, _, g, n = q.shape
    r, p = v_fb[0].shape[-2:]
    s0 = jnp.zeros((bsz, g, r, n, p), jnp.float32)
    yc_f, s_f = chunked_linear_scan(qc, kc, vc_fb[0], ac_fb[0], s0)
    yc_b, s_b = chunked_linear_scan(flip(qc), flip(kc), flip(vc_fb[1]), flip(ac_fb[1]), s0)
    y_f, _ = chunked_linear_scan(q, k, v_fb[0], a_fb[0], s_f)
    y_b, _ = chunked_linear_scan(flip(q), flip(k), flip(v_fb[1]), flip(a_fb[1]), s_b)
    return y_f + flip(y_b), yc_f + flip(yc_b)


def depthwise_conv_centred(x, w, b):
    kw, ch = w.shape
    y = lax.conv_general_dilated(x, w[:, None, :].astype(x.dtype), window_strides=(1,),
                                 padding=[(kw // 2, kw // 2)],
                                 dimension_numbers=('NWC', 'WIO', 'NWC'),
                                 feature_group_count=ch)
    return y + b


def retention_group(proj, proj_c, rope, decay_logit, norm_w, need_ctx_out):
    q, k, v, g = proj
    qc, kc, vc, gc = proj_c
    cos, sin = rope
    bsz, seq, _ = q.shape
    n_ctx = qc.shape[1]
    heads = lambda t: t.reshape(t.shape[0], t.shape[1], RET_HEADS, RET_HEAD_DIM)
    k_scale = RET_HEAD_DIM ** -0.5
    q = apply_rope_2d(heads(q), cos, sin)
    k = apply_rope_2d(heads(k), cos, sin) * k_scale
    qc = heads(qc)
    kc = heads(kc) * k_scale
    v = heads(v)[:, :, :, None, :]
    vc = heads(vc)[:, :, :, None, :]
    log_gamma = jax.nn.log_sigmoid(decay_logit.astype(jnp.float32))

    def decay(n, d):
        return jnp.broadcast_to(log_gamma[d][None, None, :, None], (bsz, n, RET_HEADS, 1))

    y, yc = bidir_scan(q, k, (v, v), (decay(seq, 0), decay(seq, 1)),
                       qc, kc, (vc, vc), (decay(n_ctx, 0), decay(n_ctx, 1)))

    def finish(y, gate):
        y = y[..., 0, :]
        y = y * lax.rsqrt(jnp.mean(y * y, axis=-1, keepdims=True) + NORM_EPS)
        y = y.reshape(y.shape[0], y.shape[1], RET_WIDTH).astype(gate.dtype) * norm_w
        return jax.nn.silu(gate) * y

    return finish(y, g), (finish(yc, gc) if need_ctx_out else None)


def ssd_group(proj, proj_c, conv_w, conv_b, dt_bias, a_log, d_skip, norm_w, need_ctx_out):
    a_neg = -jnp.exp(a_log.astype(jnp.float32)).reshape(2, SSD_GROUPS, SSD_HEADS_PER_GROUP)
    dt_b = dt_bias.astype(jnp.float32).reshape(2, SSD_GROUPS, SSD_HEADS_PER_GROUP)

    def prepare(xbc, dt):
        bsz, n, _ = xbc.shape
        xbc = jax.nn.silu(depthwise_conv_centred(xbc, conv_w, conv_b))
        xs, bm, cm = jnp.split(xbc, [SSD_WIDTH, SSD_WIDTH + SSD_GROUPS * SSD_STATE], axis=-1)
        xs = xs.reshape(bsz, n, SSD_GROUPS, SSD_HEADS_PER_GROUP, SSD_HEAD_DIM)
        bm = bm.reshape(bsz, n, SSD_GROUPS, SSD_STATE)
        cm = cm.reshape(bsz, n, SSD_GROUPS, SSD_STATE)
        dt = jax.nn.softplus(dt.astype(jnp.float32).reshape(bsz, n, 2, SSD_GROUPS, SSD_HEADS_PER_GROUP) + dt_b)
        xf = xs.astype(jnp.float32)
        v_fb = (xf * dt[:, :, 0, :, :, None], xf * dt[:, :, 1, :, :, None])
        a_fb = (dt[:, :, 0] * a_neg[0], dt[:, :, 1] * a_neg[1])
        return xf, bm, cm, v_fb, a_fb

    z, xbc, dt = proj
    zc, xbcc, dtc = proj_c
    xs, bm, cm, v_fb, a_fb = prepare(xbc, dt)
    xsc, bmc, cmc, vc_fb, ac_fb = prepare(xbcc, dtc)
    y, yc = bidir_scan(cm, bm, v_fb, a_fb, cmc, bmc, vc_fb, ac_fb)
    d = d_skip.astype(jnp.float32).reshape(SSD_GROUPS, SSD_HEADS_PER_GROUP, 1)

    def finish(y, xs, z):
        bsz, n = z.shape[:2]
        y = (y + d * xs).reshape(bsz, n, SSD_WIDTH) * jax.nn.silu(z.astype(jnp.float32))
        y = y.reshape(bsz, n, SSD_GROUPS, SSD_WIDTH // SSD_GROUPS)
        y = y * lax.rsqrt(jnp.mean(y * y, axis=-1, keepdims=True) + NORM_EPS)
        return y.reshape(bsz, n, SSD_WIDTH).astype(z.dtype) * norm_w

    return finish(y, xs, z), (finish(yc, xsc, zc) if need_ctx_out else None)


def diff_attention_group(proj, proj_c, rope, lam_vec, norm_w, lambda_init, need_ctx_out):
    q, k, v = proj
    qc, kc, vc = proj_c
    cos, sin = rope
    bsz, seq, _ = q.shape
    pair = lambda t: t.reshape(t.shape[0], t.shape[1], DIFF_HEADS, 2, DIFF_QK_DIM)
    heads = lambda t: t.reshape(t.shape[0], t.shape[1], DIFF_HEADS, DIFF_V_DIM)
    q, k, qc, kc = pair(q), pair(k), pair(qc), pair(kc)
    q1 = apply_rope_2d(q[:, :, :, 0], cos, sin)
    q2 = apply_rope_2d(q[:, :, :, 1], cos, sin)
    k_all1 = jnp.concatenate([apply_rope_2d(k[:, :, :, 0], cos, sin), kc[:, :, :, 0]], axis=1)
    k_all2 = jnp.concatenate([apply_rope_2d(k[:, :, :, 1], cos, sin), kc[:, :, :, 1]], axis=1)
    vc_h = heads(vc)
    v_all = jnp.concatenate([heads(v), vc_h], axis=1)
    lv = lam_vec.astype(jnp.float32)
    lam = jnp.exp(jnp.sum(lv[0] * lv[1])) - jnp.exp(jnp.sum(lv[2] * lv[3])) + lambda_init
    scale = DIFF_QK_DIM ** -0.5

    def attend(qa, qb, ka, kb, vv):
        s1 = jnp.einsum('bqhd,bkhd->bhqk', qa, ka, preferred_element_type=jnp.float32) * scale
        s2 = jnp.einsum('bqhd,bkhd->bhqk', qb, kb, preferred_element_type=jnp.float32) * scale
        w = jax.nn.softmax(s1, axis=-1) - lam * jax.nn.softmax(s2, axis=-1)
        return jnp.einsum('bhqk,bkhd->bqhd', w.astype(vv.dtype), vv)

    def finish(o):
        of = o.astype(jnp.float32)
        of = of * lax.rsqrt(jnp.mean(of * of, axis=-1, keepdims=True) + NORM_EPS)
        out = of.astype(o.dtype) * norm_w * (1.0 - lambda_init)
        return out.reshape(o.shape[0], o.shape[1], DIFF_WIDTH)

    nb = seq // Q_BLOCK
    blocks = lambda t: jnp.moveaxis(t.reshape(bsz, nb, Q_BLOCK, DIFF_HEADS, DIFF_QK_DIM), 1, 0)
    o = lax.map(lambda qq: attend(qq[0], qq[1], k_all1, k_all2, v_all), (blocks(q1), blocks(q2)))
    o = jnp.moveaxis(o, 0, 1).reshape(bsz, seq, DIFF_HEADS, DIFF_V_DIM)
    if need_ctx_out:
        out_c = finish(attend(qc[:, :, :, 0], qc[:, :, :, 1], kc[:, :, :, 0], kc[:, :, :, 1], vc_h))
    else:
        out_c = None
    return finish(o), out_c


def hybrid_mixer(h, hc, w_in, w_out, ret_decay_logit, ret_norm_w, ssd_conv_w, ssd_conv_b,
                 ssd_dt_bias, ssd_a_log, ssd_d, ssd_norm_w, diff_lambda, diff_norm_w,
                 rope_ret, rope_diff, lambda_init, need_ctx_out):
    parts = jnp.split(h @ w_in, IN_SPLITS, axis=-1)
    parts_c = jnp.split(hc @ w_in, IN_SPLITS, axis=-1)
    r_l, r_c = retention_group(parts[0:4], parts_c[0:4], rope_ret, ret_decay_logit, ret_norm_w, need_ctx_out)
    s_l, s_c = ssd_group(parts[4:7], parts_c[4:7], ssd_conv_w, ssd_conv_b, ssd_dt_bias, ssd_a_log,
                         ssd_d, ssd_norm_w, need_ctx_out)
    d_l, d_c = diff_attention_group(parts[7:10], parts_c[7:10], rope_diff, diff_lambda, diff_norm_w,
                                    lambda_init, need_ctx_out)
    y = jnp.concatenate([r_l, s_l, d_l], axis=-1) @ w_out
    yc = jnp.concatenate([r_c, s_c, d_c], axis=-1) @ w_out if need_ctx_out else None
    return y, yc


def swiglu(h, w_gate, w_up, w_down):
    return (jax.nn.silu(h @ w_gate) * (h @ w_up)) @ w_down


def moe_swiglu(h, router_w, w_gate, w_up, w_down):
    n_tok, d = h.shape
    n_exp = router_w.shape[-1]
    n_assign = n_tok * TOP_K
    logits = jnp.matmul(h, router_w, preferred_element_type=jnp.float32)
    top_logits, top_idx = lax.top_k(logits, TOP_K)
    gates = jax.nn.softmax(top_logits, axis=-1)
    expert = top_idx.reshape(n_assign)
    order = jnp.argsort(expert)
    expert_s = expert[order]
    token_s = order // TOP_K
    gate_s = gates.reshape(n_assign)[order]
    counts = jnp.bincount(expert, length=n_exp)
    starts = jnp.cumsum(counts) - counts
    padded = (counts + MOE_BLOCK - 1) // MOE_BLOCK * MOE_BLOCK
    pad_ends = jnp.cumsum(padded)
    pad_starts = pad_ends - padded
    dest = pad_starts[expert_s] + jnp.arange(n_assign) - starts[expert_s]
    n_blocks = -(-n_assign // MOE_BLOCK) + n_exp
    rows = jnp.zeros((n_blocks * MOE_BLOCK, d), h.dtype).at[dest].set(h[token_s])
    block_expert = jnp.minimum(jnp.searchsorted(pad_ends, jnp.arange(n_blocks) * MOE_BLOCK, side='right'), n_exp - 1)

    def expert_block(args):
        xb, e = args
        return swiglu(xb, w_gate[e], w_up[e], w_down[e])

    y_rows = lax.map(expert_block, (rows.reshape(n_blocks, MOE_BLOCK, d), block_expert))
    y_rows = y_rows.reshape(n_blocks * MOE_BLOCK, d)
    return jnp.zeros((n_tok, d), h.dtype).at[token_s].add(gate_s[:, None].astype(h.dtype) * y_rows[dest])


def setup_inputs(seed: int = 0) -> dict:
    key = jax.random.key(seed)
    ks = jax.random.split(key, 32)
    f32 = jnp.float32
    nrm = lambda k, shape, s: jax.random.normal(k, shape, f32) * s
    n_dense = (DEPTH + 1) // 2
    n_moe = DEPTH // 2
    x = nrm(ks[0], (BATCH, SEQ, D_MODEL), 1.0)
    c = nrm(ks[1], (BATCH, D_MODEL), 1.0)
    ctx = nrm(ks[2], (BATCH, CTX_LEN, D_MODEL), 1.0)
    c_ctx = nrm(ks[3], (D_MODEL,), 1.0)
    ada_w = nrm(ks[4], (DEPTH, D_MODEL, 6 * D_MODEL), ADA_SCALE * D_MODEL ** -0.5)
    ada_b = nrm(ks[5], (DEPTH, 6 * D_MODEL), 0.02)
    norm_mix_w = 1.0 + nrm(ks[6], (DEPTH, D_MODEL), 0.02)
    norm_ffn_w = 1.0 + nrm(ks[7], (DEPTH, D_MODEL), 0.02)
    w_in = nrm(ks[8], (DEPTH, D_MODEL, IN_WIDTH), D_MODEL ** -0.5)
    w_out = nrm(ks[9], (DEPTH, MIX_WIDTH, D_MODEL), MIX_WIDTH ** -0.5)
    base_logit = jnp.log(2.0 ** (5.0 + jnp.arange(RET_HEADS, dtype=f32)) - 1.0)
    ret_decay_logit = base_logit + nrm(ks[10], (DEPTH, 2, RET_HEADS), 0.1)
    ret_norm_w = 1.0 + nrm(ks[11], (DEPTH, RET_WIDTH), 0.02)
    ssd_conv_w = nrm(ks[12], (DEPTH, SSD_CONV, SSD_CONV_DIM), SSD_CONV ** -0.5)
    ssd_conv_b = nrm(ks[13], (DEPTH, SSD_CONV_DIM), 0.02)
    u = jax.random.uniform(ks[14], (DEPTH, 2, SSD_HEADS), f32)
    dt0 = jnp.exp(u * (math.log(0.1) - math.log(0.001)) + math.log(0.001))
    ssd_dt_bias = dt0 + jnp.log(-jnp.expm1(-dt0))
    ssd_a_log = jnp.log(jax.random.uniform(ks[15], (DEPTH, 2, SSD_HEADS), f32, 1.0, 16.0))
    ssd_d = 1.0 + nrm(ks[16], (DEPTH, SSD_HEADS), 0.02)
    ssd_norm_w = 1.0 + nrm(ks[17], (DEPTH, SSD_WIDTH), 0.02)
    diff_lambda = nrm(ks[18], (DEPTH, 4, DIFF_QK_DIM), 0.1)
    diff_norm_w = 1.0 + nrm(ks[19], (DEPTH, DIFF_V_DIM), 0.02)
    dense_w_gate = nrm(ks[20], (n_dense, D_MODEL, D_FF), D_MODEL ** -0.5)
    dense_w_up = nrm(ks[21], (n_dense, D_MODEL, D_FF), D_MODEL ** -0.5)
    dense_w_down = nrm(ks[22], (n_dense, D_FF, D_MODEL), D_FF ** -0.5)
    moe_router = nrm(ks[23], (n_moe, D_MODEL, N_EXPERTS), D_MODEL ** -0.5)
    moe_w_gate = nrm(ks[24], (n_moe, N_EXPERTS, D_MODEL, D_FF_EXPERT), D_MODEL ** -0.5)
    moe_w_up = nrm(ks[25], (n_moe, N_EXPERTS, D_MODEL, D_FF_EXPERT), D_MODEL ** -0.5)
    moe_w_down = nrm(ks[26], (n_moe, N_EXPERTS, D_FF_EXPERT, D_MODEL), D_FF_EXPERT ** -0.5)
    final_norm_w = 1.0 + nrm(ks[27], (D_MODEL,), 0.02)
    return {'x': x, 'c': c, 'ctx': ctx, 'c_ctx': c_ctx, 'ada_w': ada_w, 'ada_b': ada_b,
            'norm_mix_w': norm_mix_w, 'norm_ffn_w': norm_ffn_w, 'w_in': w_in, 'w_out': w_out,
            'ret_decay_logit': ret_decay_logit, 'ret_norm_w': ret_norm_w,
            'ssd_conv_w': ssd_conv_w, 'ssd_conv_b': ssd_conv_b, 'ssd_dt_bias': ssd_dt_bias,
            'ssd_a_log': ssd_a_log, 'ssd_d': ssd_d, 'ssd_norm_w': ssd_norm_w,
            'diff_lambda': diff_lambda, 'diff_norm_w': diff_norm_w,
            'dense_w_gate': dense_w_gate, 'dense_w_up': dense_w_up, 'dense_w_down': dense_w_down,
            'moe_router': moe_router, 'moe_w_gate': moe_w_gate, 'moe_w_up': moe_w_up,
            'moe_w_down': moe_w_down, 'final_norm_w': final_norm_w}


def reference(x, c, ctx, c_ctx, ada_w, ada_b, norm_mix_w, norm_ffn_w, w_in, w_out,
              ret_decay_logit, ret_norm_w, ssd_conv_w, ssd_conv_b, ssd_dt_bias, ssd_a_log,
              ssd_d, ssd_norm_w, diff_lambda, diff_norm_w, dense_w_gate, dense_w_up,
              dense_w_down, moe_router, moe_w_gate, moe_w_up, moe_w_down, final_norm_w):
    seq = x.shape[1]
    n_ctx = ctx.shape[1]
    rope_ret = rope_2d_tables(seq, RET_HEAD_DIM)
    rope_diff = rope_2d_tables(seq, DIFF_QK_DIM)
    xc = ctx
    for layer in range(DEPTH):
        last = layer == DEPTH - 1
        lambda_init = 0.8 - 0.6 * math.exp(-0.3 * layer)
        mod = jax.nn.silu(c) @ ada_w[layer] + ada_b[layer]
        mod_c = jax.nn.silu(c_ctx) @ ada_w[layer] + ada_b[layer]
        shift_m, scale_m, gate_m, shift_f, scale_f, gate_f = jnp.split(mod[:, None, :], 6, axis=-1)
        shift_mc, scale_mc, gate_mc, shift_fc, scale_fc, gate_fc = jnp.split(mod_c, 6)
        h = modulate(rmsnorm(x, norm_mix_w[layer]), shift_m, scale_m)
        hc = modulate(rmsnorm(xc, norm_mix_w[layer]), shift_mc, scale_mc)
        y, yc = hybrid_mixer(h, hc, w_in[layer], w_out[layer], ret_decay_logit[layer], ret_norm_w[layer],
                             ssd_conv_w[layer], ssd_conv_b[layer], ssd_dt_bias[layer], ssd_a_log[layer],
                             ssd_d[layer], ssd_norm_w[layer], diff_lambda[layer], diff_norm_w[layer],
                             rope_ret, rope_diff, lambda_init, not last)
        x = x + gate_m * y
        h = modulate(rmsnorm(x, norm_ffn_w[layer]), shift_f, scale_f)
        if not last:
            xc = xc + gate_mc * yc
            hc = modulate(rmsnorm(xc, norm_ffn_w[layer]), shift_fc, scale_fc)
            h = jnp.concatenate([hc, h], axis=1)
        if layer % 2 == 0:
            i = layer // 2
            f = swiglu(h, dense_w_gate[i], dense_w_up[i], dense_w_down[i])
        else:
            i = layer // 2
            f = moe_swiglu(h.reshape(-1, D_MODEL), moe_router[i], moe_w_gate[i], moe_w_up[i],
                           moe_w_down[i]).reshape(h.shape)
        if not last:
            xc = xc + gate_fc * f[:, :n_ctx]
        x = x + gate_f * f[:, f.shape[1] - seq:]
    return rmsnorm(x, final_norm_w)
```

```python
import functools
import math

import jax
import jax.numpy as jnp
import numpy as np
from jax import lax
from jax.experimental import pallas as pl
from jax.experimental.pallas import tpu as pltpu

D_MODEL = 4096
DEPTH = 2
GRID_W = 64

RET_WIDTH = D_MODEL // 4
RET_HEAD_DIM = 128
RET_HEADS = RET_WIDTH // RET_HEAD_DIM
SSD_WIDTH = D_MODEL // 2
SSD_HEAD_DIM = 64
SSD_HEADS = SSD_WIDTH // SSD_HEAD_DIM
SSD_GROUPS = 8
SSD_HEADS_PER_GROUP = SSD_HEADS // SSD_GROUPS
SSD_STATE = 128
SSD_CONV = 5
SSD_CONV_DIM = SSD_WIDTH + 2 * SSD_GROUPS * SSD_STATE
DIFF_WIDTH = D_MODEL // 4
DIFF_V_DIM = 128
DIFF_HEADS = DIFF_WIDTH // DIFF_V_DIM
DIFF_QK_DIM = DIFF_V_DIM // 2
MIX_WIDTH = RET_WIDTH + SSD_WIDTH + DIFF_WIDTH

IN_SIZES = (RET_WIDTH, RET_WIDTH, RET_WIDTH, RET_WIDTH,
            SSD_WIDTH, SSD_CONV_DIM, 2 * SSD_HEADS,
            2 * DIFF_HEADS * DIFF_QK_DIM, 2 * DIFF_HEADS * DIFF_QK_DIM, DIFF_WIDTH)
IN_WIDTH = sum(IN_SIZES)
IN_OFFS = tuple(int(s) for s in np.cumsum((0,) + IN_SIZES))
DT_WIDTH = 2 * SSD_HEADS
DT_PAD = 128
MAIN_WIDTH = IN_WIDTH - DT_WIDTH

SCAN_CHUNK = 128
Q_BLOCK = 128
ROPE_BASE = 10000.0
D_FF = 11008
D_FF_PAD = 11264
N_EXPERTS = 8
TOP_K = 2
MOE_BLOCK = 256
NORM_EPS = 1e-6

VMEM_LIMIT = 56 * 1024 * 1024


def _cparams(sem):
    return pltpu.CompilerParams(dimension_semantics=sem, vmem_limit_bytes=VMEM_LIMIT)


def _mm_body(*refs, nk, n_rhs, has_res, grouped):
    refs = list(refs)
    if grouped:
        refs.pop(0)
    a_ref = refs.pop(0)
    b_refs = [refs.pop(0) for _ in range(n_rhs)]
    if has_res:
        res_ref = refs.pop(0)
        gate_ref = refs.pop(0)
    o_ref = refs.pop(0)
    acc_refs = refs

    def epilogue(accs):
        if n_rhs == 2:
            g, u = accs
            val = (g * jax.nn.sigmoid(g)) * u
        else:
            val = accs[0]
        if has_res:
            val = res_ref[...] + gate_ref[...] * val
        o_ref[...] = val.astype(o_ref.dtype)

    a = a_ref[...]
    if b_refs[0].dtype == jnp.float32:
        precision = lax.Precision.HIGHEST
    else:
        precision = None
        a = a.astype(b_refs[0].dtype)
    prods = [jnp.dot(a, b[...], preferred_element_type=jnp.float32, precision=precision) for b in b_refs]
    if nk == 1:
        epilogue(prods)
        return
    k = pl.program_id(2)

    @pl.when(k == 0)
    def _():
        for acc, p in zip(acc_refs, prods):
            acc[...] = p

    @pl.when(k > 0)
    def _():
        for acc, p in zip(acc_refs, prods):
            acc[...] += p

    @pl.when(k == nk - 1)
    def _():
        epilogue([acc[...] for acc in acc_refs])


def _matmul(a, bs, *, tm, tn, tk, out_dtype, res=None, gate=None, rows_per_gate=None,
            block_expert=None, n_major=False):
    m, kdim = a.shape
    n = bs[0].shape[-1]
    nk = kdim // tk
    grouped = block_expert is not None
    n_rhs = len(bs)
    has_res = res is not None
    assert m % tm == 0 and n % tn == 0 and kdim % tk == 0

    if n_major:
        grid = (n // tn, m // tm, nk)
        ij = lambda g0, g1: (g1, g0)
    else:
        grid = (m // tm, n // tn, nk)
        ij = lambda g0, g1: (g0, g1)

    def a_map(g0, g1, k, *_):
        i, _j = ij(g0, g1)
        return (i, k)

    def b_map(g0, g1, k, *pref):
        i, j = ij(g0, g1)
        if grouped:
            return (pref[0][i], k, j)
        return (k, j)

    def o_map(g0, g1, k, *_):
        return ij(g0, g1)

    def gate_map(g0, g1, k, *_):
        i, j = ij(g0, g1)
        return ((i * tm) // rows_per_gate, 0, j)

    in_specs = [pl.BlockSpec((tm, tk), a_map)]
    b_block = (None, tk, tn) if grouped else (tk, tn)
    in_specs += [pl.BlockSpec(b_block, b_map) for _ in bs]
    operands = [a] + list(bs)
    if has_res:
        in_specs += [pl.BlockSpec((tm, tn), o_map), pl.BlockSpec((None, 1, tn), gate_map)]
        operands += [res, gate]
    scratch = [pltpu.VMEM((tm, tn), jnp.float32) for _ in bs] if nk > 1 else []
    body = functools.partial(_mm_body, nk=nk, n_rhs=n_rhs, has_res=has_res, grouped=grouped)
    grid_spec = pltpu.PrefetchScalarGridSpec(
        num_scalar_prefetch=1 if grouped else 0, grid=grid, in_specs=in_specs,
        out_specs=pl.BlockSpec((tm, tn), o_map), scratch_shapes=scratch)
    call = pl.pallas_call(
        body, grid_spec=grid_spec, out_shape=jax.ShapeDtypeStruct((m, n), out_dtype),
        compiler_params=_cparams(("parallel", "parallel", "arbitrary")))
    if grouped:
        return call(block_expert, *operands)
    return call(*operands)


def _ada_body(c_ref, w_ref, b_ref, o_ref):
    cond = c_ref[...]
    act = (cond * jax.nn.sigmoid(cond)).astype(jnp.bfloat16)
    o_ref[...] = jnp.dot(act, w_ref[...].astype(jnp.bfloat16),
                         preferred_element_type=jnp.float32) + b_ref[...]


def _ada_modulation(cond, w, b):
    rows, d = cond.shape
    n = w.shape[1]
    tn = 512
    return pl.pallas_call(
        _ada_body, grid=(n // tn,),
        in_specs=[pl.BlockSpec((rows, d), lambda j: (0, 0)),
                  pl.BlockSpec((d, tn), lambda j: (0, j)),
                  pl.BlockSpec((1, tn), lambda j: (0, j))],
        out_specs=pl.BlockSpec((rows, tn), lambda j: (0, j)),
        out_shape=jax.ShapeDtypeStruct((rows, n), jnp.float32),
        compiler_params=_cparams(("parallel",)))(cond, w, b)


def _norm_body(*refs, modulated):
    if modulated:
        x_ref, w_ref, shift_ref, scale_ref, o_ref = refs
    else:
        x_ref, w_ref, o_ref = refs
    x = x_ref[...]
    y = x * lax.rsqrt(jnp.mean(x * x, axis=-1, keepdims=True) + NORM_EPS) * w_ref[...]
    if modulated:
        y = y * (1.0 + scale_ref[...]) + shift_ref[...]
    o_ref[...] = y.astype(o_ref.dtype)


def _rmsnorm(x, w, *, shift=None, scale=None, rows_per_vec=None, out_dtype, tm=256):
    m, d = x.shape
    modulated = shift is not None
    in_specs = [pl.BlockSpec((tm, d), lambda i: (i, 0)), pl.BlockSpec((1, d), lambda i: (0, 0))]
    operands = [x, w]
    if modulated:
        vec = pl.BlockSpec((None, 1, d), lambda i: ((i * tm) // rows_per_vec, 0, 0))
        in_specs += [vec, vec]
        operands += [shift, scale]
    return pl.pallas_call(
        functools.partial(_norm_body, modulated=modulated), grid=(m // tm,),
        in_specs=in_specs, out_specs=pl.BlockSpec((tm, d), lambda i: (i, 0)),
        out_shape=jax.ShapeDtypeStruct((m, d), out_dtype),
        compiler_params=_cparams(("parallel",)))(*operands)


def rope_2d_tables(seq, head_dim):
    rows = seq // GRID_W
    r = jnp.broadcast_to(jnp.arange(rows, dtype=jnp.float32)[:, None], (rows, GRID_W)).reshape(seq)
    col = jnp.broadcast_to(jnp.arange(GRID_W, dtype=jnp.float32)[None, :], (rows, GRID_W)).reshape(seq)
    n_freq = head_dim // 4
    inv_freq = ROPE_BASE ** (-jnp.arange(n_freq, dtype=jnp.float32) / n_freq)
    ang_r = r[:, None] * inv_freq
    ang_c = col[:, None] * inv_freq
    ang = jnp.concatenate([ang_r, ang_r, ang_c, ang_c], axis=-1)
    return jnp.cos(ang), jnp.sin(ang)


def apply_rope_2d(t, cos, sin):
    a1, a2, b1, b2 = jnp.split(t, 4, axis=-1)
    rot = jnp.concatenate([-a2, a1, -b2, b1], axis=-1)
    return t * cos[:, None, :].astype(t.dtype) + rot * sin[:, None, :].astype(t.dtype)


def chunked_linear_scan(q, k, v, log_a, s0):
    bsz, seq, g, n = q.shape
    r, p = v.shape[-2:]
    nc = seq // SCAN_CHUNK

    def chunks(t):
        t = t.astype(jnp.float32)
        return jnp.moveaxis(t.reshape((bsz, nc, SCAN_CHUNK) + t.shape[2:]), 1, 0)

    lower = jnp.tril(jnp.ones((SCAN_CHUNK, SCAN_CHUNK), dtype=bool))[None, :, :, None, None]

    def step(s, inp):
        qc, kc, vc, ac = inp
        cs = jnp.cumsum(ac, axis=1)
        seg = cs[:, :, None] - cs[:, None, :]
        decay = jnp.exp(jnp.where(lower, seg, -jnp.inf))
        scores = jnp.einsum('bign,bjgn->bijg', qc, kc)
        y = jnp.einsum('bijg,bijgr,bjgrp->bigrp', scores, decay, vc)
        y = y + jnp.einsum('bign,bgrnp->bigrp', qc, s) * jnp.exp(cs)[..., None]
        tail = jnp.exp(cs[:, -1:] - cs)
        s = jnp.exp(cs[:, -1])[..., None, None] * s + jnp.einsum('bjgn,bjgr,bjgrp->bgrnp', kc, tail, vc)
        return s, y

    s, y = lax.scan(step, s0.astype(jnp.float32), (chunks(q), chunks(k), chunks(v), chunks(log_a)))
    y = jnp.moveaxis(y, 0, 1).reshape(bsz, seq, g, r, p)
    return y, s


def bidir_scan(q, k, v_fb, a_fb, qc, kc, vc_fb, ac_fb):
    flip = lambda t: jnp.flip(t, axis=1)
    bsz, _, g, n = q.shape
    r, p = v_fb[0].shape[-2:]
    s0 = jnp.zeros((bsz, g, r, n, p), jnp.float32)
    yc_f, s_f = chunked_linear_scan(qc, kc, vc_fb[0], ac_fb[0], s0)
    yc_b, s_b = chunked_linear_scan(flip(qc), flip(kc), flip(vc_fb[1]), flip(ac_fb[1]), s0)
    y_f, _ = chunked_linear_scan(q, k, v_fb[0], a_fb[0], s_f)
    y_b, _ = chunked_linear_scan(flip(q), flip(k), flip(v_fb[1]), flip(a_fb[1]), s_b)
    return y_f + flip(y_b), yc_f + flip(yc_b)


def depthwise_conv_centred(x, w, b):
    kw, ch = w.shape
    y = lax.conv_general_dilated(x, w[:, None, :].astype(x.dtype), window_strides=(1,),
                                 padding=[(kw // 2, kw // 2)],
                                 dimension_numbers=('NWC', 'WIO', 'NWC'),
                                 feature_group_count=ch)
    return y + b


def retention_group(proj, proj_c, rope, decay_logit, norm_w, need_ctx_out):
    q, k, v, g = proj
    qc, kc, vc, gc = proj_c
    cos, sin = rope
    bsz, seq, _ = q.shape
    n_ctx = qc.shape[1]
    heads = lambda t: t.reshape(t.shape[0], t.shape[1], RET_HEADS, RET_HEAD_DIM)
    k_scale = RET_HEAD_DIM ** -0.5
    q = apply_rope_2d(heads(q), cos, sin)
    k = apply_rope_2d(heads(k), cos, sin) * k_scale
    qc = heads(qc)
    kc = heads(kc) * k_scale
    v = heads(v)[:, :, :, None, :]
    vc = heads(vc)[:, :, :, None, :]
    log_gamma = jax.nn.log_sigmoid(decay_logit.astype(jnp.float32))

    def decay(n, d):
        return jnp.broadcast_to(log_gamma[d][None, None, :, None], (bsz, n, RET_HEADS, 1))

    y, yc = bidir_scan(q, k, (v, v), (decay(seq, 0), decay(seq, 1)),
                       qc, kc, (vc, vc), (decay(n_ctx, 0), decay(n_ctx, 1)))

    def finish(y, gate):
        y = y[..., 0, :]
        y = y * lax.rsqrt(jnp.mean(y * y, axis=-1, keepdims=True) + NORM_EPS)
        y = y.reshape(y.shape[0], y.shape[1], RET_WIDTH).astype(gate.dtype) * norm_w
        return jax.nn.silu(gate) * y

    return finish(y, g), (finish(yc, gc) if need_ctx_out else None)


def ssd_group(proj, proj_c, conv_w, conv_b, dt_bias, a_log, d_skip, norm_w, need_ctx_out):
    a_neg = -jnp.exp(a_log.astype(jnp.float32)).reshape(2, SSD_GROUPS, SSD_HEADS_PER_GROUP)
    dt_b = dt_bias.astype(jnp.float32).reshape(2, SSD_GROUPS, SSD_HEADS_PER_GROUP)

    def prepare(xbc, dt):
        bsz, n, _ = xbc.shape
        xbc = jax.nn.silu(depthwise_conv_centred(xbc, conv_w, conv_b))
        xs, bm, cm = jnp.split(xbc, [SSD_WIDTH, SSD_WIDTH + SSD_GROUPS * SSD_STATE], axis=-1)
        xs = xs.reshape(bsz, n, SSD_GROUPS, SSD_HEADS_PER_GROUP, SSD_HEAD_DIM)
        bm = bm.reshape(bsz, n, SSD_GROUPS, SSD_STATE)
        cm = cm.reshape(bsz, n, SSD_GROUPS, SSD_STATE)
        dt = jax.nn.softplus(dt.astype(jnp.float32).reshape(bsz, n, 2, SSD_GROUPS, SSD_HEADS_PER_GROUP) + dt_b)
        xf = xs.astype(jnp.float32)
        v_fb = (xf * dt[:, :, 0, :, :, None], xf * dt[:, :, 1, :, :, None])
        a_fb = (dt[:, :, 0] * a_neg[0], dt[:, :, 1] * a_neg[1])
        return xf, bm, cm, v_fb, a_fb

    z, xbc, dt = proj
    zc, xbcc, dtc = proj_c
    xs, bm, cm, v_fb, a_fb = prepare(xbc, dt)
    xsc, bmc, cmc, vc_fb, ac_fb = prepare(xbcc, dtc)
    y, yc = bidir_scan(cm, bm, v_fb, a_fb, cmc, bmc, vc_fb, ac_fb)
    d = d_skip.astype(jnp.float32).reshape(SSD_GROUPS, SSD_HEADS_PER_GROUP, 1)

    def finish(y, xs, z):
        bsz, n = z.shape[:2]
        y = (y + d * xs).reshape(bsz, n, SSD_WIDTH) * jax.nn.silu(z.astype(jnp.float32))
        y = y.reshape(bsz, n, SSD_GROUPS, SSD_WIDTH // SSD_GROUPS)
        y = y * lax.rsqrt(jnp.mean(y * y, axis=-1, keepdims=True) + NORM_EPS)
        return y.reshape(bsz, n, SSD_WIDTH).astype(z.dtype) * norm_w

    return finish(y, xs, z), (finish(yc, xsc, zc) if need_ctx_out else None)


def diff_attention_group(proj, proj_c, rope, lam_vec, norm_w, lambda_init, need_ctx_out):
    q, k, v = proj
    qc, kc, vc = proj_c
    cos, sin = rope
    bsz, seq, _ = q.shape
    pair = lambda t: t.reshape(t.shape[0], t.shape[1], DIFF_HEADS, 2, DIFF_QK_DIM)
    heads = lambda t: t.reshape(t.shape[0], t.shape[1], DIFF_HEADS, DIFF_V_DIM)
    q, k, qc, kc = pair(q), pair(k), pair(qc), pair(kc)
    q1 = apply_rope_2d(q[:, :, :, 0], cos, sin)
    q2 = apply_rope_2d(q[:, :, :, 1], cos, sin)
    k_all1 = jnp.concatenate([apply_rope_2d(k[:, :, :, 0], cos, sin), kc[:, :, :, 0]], axis=1)
    k_all2 = jnp.concatenate([apply_rope_2d(k[:, :, :, 1], cos, sin), kc[:, :, :, 1]], axis=1)
    vc_h = heads(vc)
    v_all = jnp.concatenate([heads(v), vc_h], axis=1)
    lv = lam_vec.astype(jnp.float32)
    lam = jnp.exp(jnp.sum(lv[0] * lv[1])) - jnp.exp(jnp.sum(lv[2] * lv[3])) + lambda_init
    scale = DIFF_QK_DIM ** -0.5

    def attend(qa, qb, ka, kb, vv):
        s1 = jnp.einsum('bqhd,bkhd->bhqk', qa, ka, preferred_element_type=jnp.float32) * scale
        s2 = jnp.einsum('bqhd,bkhd->bhqk', qb, kb, preferred_element_type=jnp.float32) * scale
        w = jax.nn.softmax(s1, axis=-1) - lam * jax.nn.softmax(s2, axis=-1)
        return jnp.einsum('bhqk,bkhd->bqhd', w.astype(vv.dtype), vv)

    def finish(o):
        of = o.astype(jnp.float32)
        of = of * lax.rsqrt(jnp.mean(of * of, axis=-1, keepdims=True) + NORM_EPS)
        out = of.astype(o.dtype) * norm_w * (1.0 - lambda_init)
        return out.reshape(o.shape[0], o.shape[1], DIFF_WIDTH)

    nb = seq // Q_BLOCK
    blocks = lambda t: jnp.moveaxis(t.reshape(bsz, nb, Q_BLOCK, DIFF_HEADS, DIFF_QK_DIM), 1, 0)
    o = lax.map(lambda qq: attend(qq[0], qq[1], k_all1, k_all2, v_all), (blocks(q1), blocks(q2)))
    o = jnp.moveaxis(o, 0, 1).reshape(bsz, seq, DIFF_HEADS, DIFF_V_DIM)
    if need_ctx_out:
        out_c = finish(attend(qc[:, :, :, 0], qc[:, :, :, 1], kc[:, :, :, 0], kc[:, :, :, 1], vc_h))
    else:
        out_c = None
    return finish(o), out_c


def _split_proj(main, dt, bsz):
    n = main.shape[0] // bsz
    sizes = [s for idx, s in enumerate(IN_SIZES) if idx != 6]
    offs = np.cumsum([0] + sizes)
    parts = [main[:, offs[i]:offs[i + 1]].reshape(bsz, n, sizes[i]) for i in range(len(sizes))]
    parts.insert(6, dt[:, :DT_WIDTH].reshape(bsz, n, DT_WIDTH))
    return parts


def _mixers(parts, parts_c, lw, rope_ret, rope_diff, lambda_init, need_ctx_out):
    r_l, r_c = retention_group(parts[0:4], parts_c[0:4], rope_ret, lw['ret_decay_logit'],
                               lw['ret_norm_w'], need_ctx_out)
    s_l, s_c = ssd_group(parts[4:7], parts_c[4:7], lw['ssd_conv_w'], lw['ssd_conv_b'],
                         lw['ssd_dt_bias'], lw['ssd_a_log'], lw['ssd_d'], lw['ssd_norm_w'], need_ctx_out)
    d_l, d_c = diff_attention_group(parts[7:10], parts_c[7:10], rope_diff, lw['diff_lambda'],
                                    lw['diff_norm_w'], lambda_init, need_ctx_out)
    mix = jnp.concatenate([r_l, s_l, d_l], axis=-1)
    mix_c = jnp.concatenate([r_c, s_c, d_c], axis=-1) if need_ctx_out else None
    return mix, mix_c


def _gather_rows_body(idx_ref, src_hbm, o_ref, sem, *, tm):
    base = pl.program_id(0) * tm

    def row_copy(r):
        return pltpu.make_async_copy(src_hbm.at[pl.ds(idx_ref[base + r], 1)], o_ref.at[pl.ds(r, 1)], sem)

    def start(r, carry):
        row_copy(r).start()
        return carry

    def wait(r, carry):
        row_copy(r).wait()
        return carry

    lax.fori_loop(0, tm, start, 0)
    lax.fori_loop(0, tm, wait, 0)


def _gather_rows(idx, src, *, tm):
    n_rows = idx.shape[0]
    d = src.shape[1]
    grid_spec = pltpu.PrefetchScalarGridSpec(
        num_scalar_prefetch=1, grid=(n_rows // tm,),
        in_specs=[pl.BlockSpec(memory_space=pl.ANY)],
        out_specs=pl.BlockSpec((tm, d), lambda i, idx_ref: (i, 0)),
        scratch_shapes=[pltpu.SemaphoreType.DMA(())])
    return pl.pallas_call(
        functools.partial(_gather_rows_body, tm=tm), grid_spec=grid_spec,
        out_shape=jax.ShapeDtypeStruct((n_rows, d), src.dtype),
        compiler_params=_cparams(("arbitrary",)))(idx, src)


def _moe_combine_body(pos_ref, y_hbm, g_ref, res_ref, gate_ref, o_ref, buf, sem, *, tm, n_steps):
    i = pl.program_id(0)

    def row_copy(step, slot, r, kk):
        src_row = pos_ref[(step * tm + r) * TOP_K + kk]
        return pltpu.make_async_copy(y_hbm.at[pl.ds(src_row, 1)], buf.at[slot, kk, pl.ds(r, 1)], sem.at[slot])

    def start_block(step, slot):
        def body(r, carry):
            for kk in range(TOP_K):
                row_copy(step, slot, r, kk).start()
            return carry
        lax.fori_loop(0, tm, body, 0)

    def wait_block(step, slot):
        def body(r, carry):
            for kk in range(TOP_K):
                row_copy(step, slot, r, kk).wait()
            return carry
        lax.fori_loop(0, tm, body, 0)

    slot = i % 2

    @pl.when(i == 0)
    def _():
        start_block(0, 0)

    @pl.when(i + 1 < n_steps)
    def _():
        start_block(i + 1, 1 - slot)

    wait_block(i, slot)
    g = g_ref[...]
    f = g[:, 0:1] * buf[slot, 0]
    for kk in range(1, TOP_K):
        f = f + g[:, kk:kk + 1] * buf[slot, kk]
    o_ref[...] = res_ref[...] + gate_ref[...] * f


def _moe_combine(pos, y_rows, gates, res, gate_vec, rows_per_gate, *, tm=128):
    n_tok, d = res.shape
    n_steps = n_tok // tm
    grid_spec = pltpu.PrefetchScalarGridSpec(
        num_scalar_prefetch=1, grid=(n_steps,),
        in_specs=[pl.BlockSpec(memory_space=pl.ANY),
                  pl.BlockSpec((tm, TOP_K), lambda i, p: (i, 0)),
                  pl.BlockSpec((tm, d), lambda i, p: (i, 0)),
                  pl.BlockSpec((None, 1, d), lambda i, p: ((i * tm) // rows_per_gate, 0, 0))],
        out_specs=pl.BlockSpec((tm, d), lambda i, p: (i, 0)),
        scratch_shapes=[pltpu.VMEM((2, TOP_K, tm, d), jnp.float32), pltpu.SemaphoreType.DMA((2,))])
    return pl.pallas_call(
        functools.partial(_moe_combine_body, tm=tm, n_steps=n_steps), grid_spec=grid_spec,
        out_shape=jax.ShapeDtypeStruct((n_tok, d), jnp.float32),
        compiler_params=_cparams(("arbitrary",)))(pos, y_rows, gates, res, gate_vec)


def _moe(h, router_w, w_gate, w_up, w_down, res, gate_vec, rows_per_gate):
    n_tok, d = h.shape
    n_exp = N_EXPERTS
    n_assign = n_tok * TOP_K
    logits = _matmul(h, [router_w], tm=512, tn=128, tk=d, out_dtype=jnp.float32)[:, :n_exp]
    top_logits, top_idx = lax.top_k(logits, TOP_K)
    gates = jax.nn.softmax(top_logits, axis=-1)
    expert = top_idx.reshape(n_assign)
    order = jnp.argsort(expert)
    expert_s = expert[order]
    token_s = order // TOP_K
    counts = jnp.bincount(expert, length=n_exp)
    starts = jnp.cumsum(counts) - counts
    padded = (counts + MOE_BLOCK - 1) // MOE_BLOCK * MOE_BLOCK
    pad_ends = jnp.cumsum(padded)
    pad_starts = pad_ends - padded
    dest = pad_starts[expert_s] + jnp.arange(n_assign) - starts[expert_s]
    n_blocks = -(-n_assign // MOE_BLOCK) + n_exp
    src_tok = jnp.zeros((n_blocks * MOE_BLOCK,), jnp.int32).at[dest].set(token_s.astype(jnp.int32))
    rows = _gather_rows(src_tok, h, tm=MOE_BLOCK)
    block_expert = jnp.minimum(
        jnp.searchsorted(pad_ends, jnp.arange(n_blocks) * MOE_BLOCK, side='right'), n_exp - 1
    ).astype(jnp.int32)
    t = _matmul(rows, [w_gate, w_up], tm=MOE_BLOCK, tn=512, tk=d, out_dtype=jnp.bfloat16,
                block_expert=block_expert, n_major=True)
    y_rows = _matmul(t, [w_down], tm=MOE_BLOCK, tn=1024, tk=w_down.shape[1], out_dtype=jnp.float32,
                     block_expert=block_expert, n_major=True)
    pos = jnp.zeros((n_assign,), jnp.int32).at[order].set(dest.astype(jnp.int32))
    return _moe_combine(pos, y_rows, gates, res, gate_vec, rows_per_gate)


def kernel(x, c, ctx, c_ctx, ada_w, ada_b, norm_mix_w, norm_ffn_w, w_in, w_out, ret_decay_logit, ret_norm_w, ssd_conv_w, ssd_conv_b, ssd_dt_bias, ssd_a_log, ssd_d, ssd_norm_w, diff_lambda, diff_norm_w, dense_w_gate, dense_w_up, dense_w_down, moe_router, moe_w_gate, moe_w_up, moe_w_down, final_norm_w):
    bsz, seq, d = x.shape
    n_ctx = ctx.shape[1]
    bf16 = jnp.bfloat16
    rope_ret = rope_2d_tables(seq, RET_HEAD_DIM)
    rope_diff = rope_2d_tables(seq, DIFF_QK_DIM)

    xl = x.reshape(bsz * seq, d)
    xc = ctx.reshape(bsz * n_ctx, d)
    cond = jnp.concatenate([c, c_ctx[None, :], jnp.zeros((8 - bsz - 1, d), jnp.float32)], axis=0)

    for layer in range(DEPTH):
        last = layer == DEPTH - 1
        lambda_init = 0.8 - 0.6 * math.exp(-0.3 * layer)
        lw = {'ret_decay_logit': ret_decay_logit[layer], 'ret_norm_w': ret_norm_w[layer],
              'ssd_conv_w': ssd_conv_w[layer], 'ssd_conv_b': ssd_conv_b[layer],
              'ssd_dt_bias': ssd_dt_bias[layer], 'ssd_a_log': ssd_a_log[layer], 'ssd_d': ssd_d[layer],
              'ssd_norm_w': ssd_norm_w[layer], 'diff_lambda': diff_lambda[layer],
              'diff_norm_w': diff_norm_w[layer]}

        mod = _ada_modulation(cond, ada_w[layer], ada_b[layer][None, :])
        mod_l = mod[:bsz].reshape(bsz, 1, 6, d)
        mod_c = mod[bsz:bsz + 1].reshape(1, 1, 6, d)
        shift_m, scale_m, gate_m, shift_f, scale_f, gate_f = [mod_l[:, :, i] for i in range(6)]
        shift_mc, scale_mc, gate_mc, shift_fc, scale_fc, gate_fc = [mod_c[:, :, i] for i in range(6)]

        w_in_l = w_in[layer]
        w_main = jnp.concatenate([w_in_l[:, :IN_OFFS[6]], w_in_l[:, IN_OFFS[7]:]], axis=1).astype(bf16)
        w_dt = jnp.pad(w_in_l[:, IN_OFFS[6]:IN_OFFS[7]], ((0, 0), (0, DT_PAD - DT_WIDTH))).astype(bf16)
        w_out_l = w_out[layer].astype(bf16)
        nmw = norm_mix_w[layer][None, :]
        nfw = norm_ffn_w[layer][None, :]

        h = _rmsnorm(xl, nmw, shift=shift_m, scale=scale_m, rows_per_vec=seq, out_dtype=bf16)
        hc = _rmsnorm(xc, nmw, shift=shift_mc, scale=scale_mc, rows_per_vec=bsz * n_ctx, out_dtype=bf16)
        proj = _matmul(h, [w_main], tm=1024, tn=1024, tk=d, out_dtype=jnp.float32)
        proj_dt = _matmul(h, [w_dt], tm=1024, tn=DT_PAD, tk=d, out_dtype=jnp.float32)
        proj_c = _matmul(hc, [w_main], tm=512, tn=1024, tk=d, out_dtype=jnp.float32)
        proj_c_dt = _matmul(hc, [w_dt], tm=512, tn=DT_PAD, tk=d, out_dtype=jnp.float32)
        mix, mix_c = _mixers(_split_proj(proj, proj_dt, bsz), _split_proj(proj_c, proj_c_dt, bsz),
                             lw, rope_ret, rope_diff, lambda_init, not last)
        mix = mix.reshape(bsz * seq, MIX_WIDTH).astype(bf16)
        xl = _matmul(mix, [w_out_l], tm=1024, tn=1024, tk=MIX_WIDTH // 2, out_dtype=jnp.float32,
                     res=xl, gate=gate_m, rows_per_gate=seq)
        if not last:
            mix_c = mix_c.reshape(bsz * n_ctx, MIX_WIDTH).astype(bf16)
            xc = _matmul(mix_c, [w_out_l], tm=512, tn=1024, tk=MIX_WIDTH // 2, out_dtype=jnp.float32,
                         res=xc, gate=gate_mc, rows_per_gate=bsz * n_ctx)

        h = _rmsnorm(xl, nfw, shift=shift_f, scale=scale_f, rows_per_vec=seq,
                     out_dtype=bf16 if layer % 2 == 0 else jnp.float32)
        if not last:
            hc = _rmsnorm(xc, nfw, shift=shift_fc, scale=scale_fc, rows_per_vec=bsz * n_ctx, out_dtype=bf16)
        i = layer // 2
        if layer % 2 == 0:
            pad_f = D_FF_PAD - D_FF
            wg = jnp.pad(dense_w_gate[i], ((0, 0), (0, pad_f))).astype(bf16)
            wu = jnp.pad(dense_w_up[i], ((0, 0), (0, pad_f))).astype(bf16)
            wd = jnp.pad(dense_w_down[i], ((0, pad_f), (0, 0))).astype(bf16)
            t = _matmul(h, [wg, wu], tm=1024, tn=512, tk=d, out_dtype=bf16)
            xl = _matmul(t, [wd], tm=1024, tn=1024, tk=2816, out_dtype=jnp.float32,
                         res=xl, gate=gate_f, rows_per_gate=seq)
            if not last:
                tc = _matmul(hc, [wg, wu], tm=512, tn=512, tk=d, out_dtype=bf16)
                xc = _matmul(tc, [wd], tm=512, tn=1024, tk=2816, out_dtype=jnp.float32,
                             res=xc, gate=gate_fc, rows_per_gate=bsz * n_ctx)
        else:
            router = jnp.pad(moe_router[i], ((0, 0), (0, 128 - N_EXPERTS)))
            wg, wu, wd = moe_w_gate[i].astype(bf16), moe_w_up[i].astype(bf16), moe_w_down[i].astype(bf16)
            assert last, "expert layers that still carry a context stream are not supported"
            xl = _moe(h, router, wg, wu, wd, xl, gate_f, seq)

    out = _rmsnorm(xl, final_norm_w[None, :], out_dtype=jnp.float32)
    return out.reshape(bsz, seq, d)
```

```python
import functools
import math

import jax
import jax.numpy as jnp
import numpy as np
from jax import lax
from jax.experimental import pallas as pl
from jax.experimental.pallas import tpu as pltpu

D_MODEL = 4096
DEPTH = 2
GRID_W = 64

RET_WIDTH = D_MODEL // 4
RET_HEAD_DIM = 128
RET_HEADS = RET_WIDTH // RET_HEAD_DIM
SSD_WIDTH = D_MODEL // 2
SSD_HEAD_DIM = 64
SSD_HEADS = SSD_WIDTH // SSD_HEAD_DIM
SSD_GROUPS = 8
SSD_HEADS_PER_GROUP = SSD_HEADS // SSD_GROUPS
SSD_STATE = 128
SSD_CONV = 5
SSD_CONV_DIM = SSD_WIDTH + 2 * SSD_GROUPS * SSD_STATE
DIFF_WIDTH = D_MODEL // 4
DIFF_V_DIM = 128
DIFF_HEADS = DIFF_WIDTH // DIFF_V_DIM
DIFF_QK_DIM = DIFF_V_DIM // 2
MIX_WIDTH = RET_WIDTH + SSD_WIDTH + DIFF_WIDTH

IN_SIZES = (RET_WIDTH, RET_WIDTH, RET_WIDTH, RET_WIDTH,
            SSD_WIDTH, SSD_CONV_DIM, 2 * SSD_HEADS,
            2 * DIFF_HEADS * DIFF_QK_DIM, 2 * DIFF_HEADS * DIFF_QK_DIM, DIFF_WIDTH)
IN_WIDTH = sum(IN_SIZES)
IN_OFFS = tuple(int(s) for s in np.cumsum((0,) + IN_SIZES))
DT_WIDTH = 2 * SSD_HEADS
LANE = 128
DT_PAD = LANE
MAIN_WIDTH = IN_WIDTH - DT_WIDTH

COL_RET_Q, COL_RET_K, COL_RET_V, COL_RET_G = 0, 8, 16, 24
COL_SSD_Z = 32
COL_SSD_X, COL_SSD_B, COL_SSD_C = 48, 64, 72
COL_DIFF_Q, COL_DIFF_K, COL_DIFF_V = 80, 88, 96

KV_CHUNK = 256
SCAN_T = 256
GROUP_W = SSD_HEADS_PER_GROUP * SSD_HEAD_DIM
N_DIR_HEADS = 2 * SSD_HEADS_PER_GROUP

ROPE_BASE = 10000.0
D_FF = 11008
D_FF_PAD = 11264
N_EXPERTS = 8
TOP_K = 2
MOE_BLOCK = 256
NORM_EPS = 1e-6

VMEM_LIMIT = 56 * 1024 * 1024


def _cparams(sem):
    return pltpu.CompilerParams(dimension_semantics=sem, vmem_limit_bytes=VMEM_LIMIT)


def _silu(x):
    return x * jax.nn.sigmoid(x)


def _softplus(x):
    return jnp.maximum(x, 0.0) + jnp.log(1.0 + jnp.exp(-jnp.abs(x)))


def _f32_dot(a, b):
    return jnp.dot(a, b, preferred_element_type=jnp.float32, precision=lax.Precision.HIGHEST)


def _bf16_dot(a, b):
    return jnp.dot(a.astype(jnp.bfloat16), b.astype(jnp.bfloat16), preferred_element_type=jnp.float32)


def _mm_body(*refs, nk, n_rhs, has_res, grouped):
    refs = list(refs)
    if grouped:
        refs.pop(0)
    a_ref = refs.pop(0)
    b_refs = [refs.pop(0) for _ in range(n_rhs)]
    if has_res:
        res_ref = refs.pop(0)
        gate_ref = refs.pop(0)
    o_ref = refs.pop(0)
    acc_refs = refs

    def epilogue(accs):
        if n_rhs == 2:
            g, u = accs
            val = _silu(g) * u
        else:
            val = accs[0]
        if has_res:
            val = res_ref[...] + gate_ref[...] * val
        o_ref[...] = val.astype(o_ref.dtype)

    a = a_ref[...]
    if b_refs[0].dtype == jnp.float32:
        precision = lax.Precision.HIGHEST
    else:
        precision = None
        a = a.astype(b_refs[0].dtype)
    prods = [jnp.dot(a, b[...], preferred_element_type=jnp.float32, precision=precision) for b in b_refs]
    if nk == 1:
        epilogue(prods)
        return
    k = pl.program_id(2)

    @pl.when(k == 0)
    def _():
        for acc, p in zip(acc_refs, prods):
            acc[...] = p

    @pl.when(k > 0)
    def _():
        for acc, p in zip(acc_refs, prods):
            acc[...] += p

    @pl.when(k == nk - 1)
    def _():
        epilogue([acc[...] for acc in acc_refs])


def _matmul(a, bs, *, tm, tn, tk, out_dtype, res=None, gate=None, rows_per_gate=None,
            block_expert=None, n_major=False, name="matmul"):
    m, kdim = a.shape
    n = bs[0].shape[-1]
    nk = kdim // tk
    grouped = block_expert is not None
    n_rhs = len(bs)
    has_res = res is not None
    assert m % tm == 0 and n % tn == 0 and kdim % tk == 0

    if n_major:
        grid = (n // tn, m // tm, nk)
        ij = lambda g0, g1: (g1, g0)
    else:
        grid = (m // tm, n // tn, nk)
        ij = lambda g0, g1: (g0, g1)

    def a_map(g0, g1, k, *_):
        i, _j = ij(g0, g1)
        return (i, k)

    def b_map(g0, g1, k, *pref):
        i, j = ij(g0, g1)
        if grouped:
            return (pref[0][i], k, j)
        return (k, j)

    def o_map(g0, g1, k, *_):
        return ij(g0, g1)

    def gate_map(g0, g1, k, *_):
        i, j = ij(g0, g1)
        return ((i * tm) // rows_per_gate, 0, j)

    in_specs = [pl.BlockSpec((tm, tk), a_map)]
    b_block = (None, tk, tn) if grouped else (tk, tn)
    in_specs += [pl.BlockSpec(b_block, b_map) for _ in bs]
    operands = [a] + list(bs)
    if has_res:
        in_specs += [pl.BlockSpec((tm, tn), o_map), pl.BlockSpec((None, 1, tn), gate_map)]
        operands += [res, gate]
    scratch = [pltpu.VMEM((tm, tn), jnp.float32) for _ in bs] if nk > 1 else []
    body = functools.partial(_mm_body, nk=nk, n_rhs=n_rhs, has_res=has_res, grouped=grouped)
    grid_spec = pltpu.PrefetchScalarGridSpec(
        num_scalar_prefetch=1 if grouped else 0, grid=grid, in_specs=in_specs,
        out_specs=pl.BlockSpec((tm, tn), o_map), scratch_shapes=scratch)
    call = pl.pallas_call(
        body, grid_spec=grid_spec, out_shape=jax.ShapeDtypeStruct((m, n), out_dtype),
        compiler_params=_cparams(("parallel", "parallel", "arbitrary")), name=name)
    if grouped:
        return call(block_expert, *operands)
    return call(*operands)


def _ada_body(c_ref, w_ref, b_ref, o_ref):
    act = _silu(c_ref[...]).astype(jnp.bfloat16)
    o_ref[...] = jnp.dot(act, w_ref[...].astype(jnp.bfloat16),
                         preferred_element_type=jnp.float32) + b_ref[...]


def _ada_modulation(cond, w, b):
    rows, d = cond.shape
    n = w.shape[1]
    tn = 512
    return pl.pallas_call(
        _ada_body, grid=(n // tn,),
        in_specs=[pl.BlockSpec((rows, d), lambda j: (0, 0)),
                  pl.BlockSpec((d, tn), lambda j: (0, j)),
                  pl.BlockSpec((1, tn), lambda j: (0, j))],
        out_specs=pl.BlockSpec((rows, tn), lambda j: (0, j)),
        out_shape=jax.ShapeDtypeStruct((rows, n), jnp.float32),
        compiler_params=_cparams(("parallel",)), name="ada_modulation")(cond, w, b)


def _norm_body(*refs, modulated):
    if modulated:
        x_ref, w_ref, shift_ref, scale_ref, o_ref = refs
    else:
        x_ref, w_ref, o_ref = refs
    x = x_ref[...]
    y = x * lax.rsqrt(jnp.mean(x * x, axis=-1, keepdims=True) + NORM_EPS) * w_ref[...]
    if modulated:
        y = y * (1.0 + scale_ref[...]) + shift_ref[...]
    o_ref[...] = y.astype(o_ref.dtype)


def _rmsnorm(x, w, *, shift=None, scale=None, rows_per_vec=None, out_dtype, tm=256):
    m, d = x.shape
    modulated = shift is not None
    in_specs = [pl.BlockSpec((tm, d), lambda i: (i, 0)), pl.BlockSpec((1, d), lambda i: (0, 0))]
    operands = [x, w]
    if modulated:
        vec = pl.BlockSpec((None, 1, d), lambda i: ((i * tm) // rows_per_vec, 0, 0))
        in_specs += [vec, vec]
        operands += [shift, scale]
    return pl.pallas_call(
        functools.partial(_norm_body, modulated=modulated), grid=(m // tm,),
        in_specs=in_specs, out_specs=pl.BlockSpec((tm, d), lambda i: (i, 0)),
        out_shape=jax.ShapeDtypeStruct((m, d), out_dtype),
        compiler_params=_cparams(("parallel",)), name="rmsnorm")(*operands)


def _rope_tables(seq, head_dim, reps):
    rows = seq // GRID_W
    r = jnp.broadcast_to(jnp.arange(rows, dtype=jnp.float32)[:, None], (rows, GRID_W)).reshape(seq)
    col = jnp.broadcast_to(jnp.arange(GRID_W, dtype=jnp.float32)[None, :], (rows, GRID_W)).reshape(seq)
    n_freq = head_dim // 4
    inv_freq = ROPE_BASE ** (-jnp.arange(n_freq, dtype=jnp.float32) / n_freq)
    ang_r = r[:, None] * inv_freq
    ang_c = col[:, None] * inv_freq
    ang = jnp.concatenate([ang_r, ang_r, ang_c, ang_c], axis=-1)
    sign = jnp.concatenate([-jnp.ones(n_freq), jnp.ones(n_freq), -jnp.ones(n_freq), jnp.ones(n_freq)])
    return jnp.tile(jnp.cos(ang), (1, reps)), jnp.tile(jnp.sin(ang) * sign, (1, reps))


def _rope(t, cos, sin_signed, quarter):
    lane = lax.broadcasted_iota(jnp.int32, t.shape, 1)
    width = t.shape[1]
    partner = jnp.where(lane % (2 * quarter) < quarter,
                        pltpu.roll(t, width - quarter, 1), pltpu.roll(t, quarter, 1))
    return t * cos + partner * sin_signed


def _ssd_conv_body(x_ref, w_ref, b_ref, o_ref, *, seq, transpose, chunk):
    w = w_ref[...]
    bias = b_ref[...]
    half = SSD_CONV // 2
    for s in range(0, seq, chunk):
        lo, hi = max(s - 8, 0), min(s + chunk + 8, seq)
        xa = x_ref[lo:hi, :]
        n = hi - lo
        t_glob = lax.broadcasted_iota(jnp.int32, (n, 1), 0) + lo
        acc = xa * w[half:half + 1, :]
        for kk in range(SSD_CONV):
            d = kk - half
            if d == 0:
                continue
            shifted = pltpu.roll(xa, (-d) % n, 0)
            valid = jnp.logical_and(t_glob + d >= 0, t_glob + d < seq)
            acc = acc + jnp.where(valid, shifted, 0.0) * w[kk:kk + 1, :]
        y = _silu(acc + bias)[s - lo:s - lo + chunk, :]
        if transpose:
            o_ref[:, s:s + chunk] = y.T.astype(o_ref.dtype)
        else:
            o_ref[s:s + chunk, :] = y.astype(o_ref.dtype)


def _ssd_conv(proj, conv_w, conv_b, *, bsz, seq, col0, ncols, conv_col0, transpose, out_dtype):
    chunk = min(512, seq)
    if transpose:
        out_shape = jax.ShapeDtypeStruct((bsz, ncols * LANE, seq), out_dtype)
        out_spec = pl.BlockSpec((None, LANE, seq), lambda b, j: (b, j, 0))
    else:
        out_shape = jax.ShapeDtypeStruct((bsz * seq, ncols * LANE), out_dtype)
        out_spec = pl.BlockSpec((seq, LANE), lambda b, j: (b, j))
    return pl.pallas_call(
        functools.partial(_ssd_conv_body, seq=seq, transpose=transpose, chunk=chunk),
        grid=(bsz, ncols),
        in_specs=[pl.BlockSpec((seq, LANE), lambda b, j: (b, col0 + j)),
                  pl.BlockSpec((SSD_CONV, LANE), lambda b, j: (0, conv_col0 + j)),
                  pl.BlockSpec((1, LANE), lambda b, j: (0, conv_col0 + j))],
        out_specs=out_spec, out_shape=out_shape,
        compiler_params=_cparams(("parallel", "parallel")), name="ssd_conv")(proj, conv_w, conv_b)


def _upper_ones(n):
    r = lax.broadcasted_iota(jnp.int32, (n, n), 0)
    c = lax.broadcasted_iota(jnp.int32, (n, n), 1)
    return (r <= c).astype(jnp.float32)


def _head_lane_select(vals, width, shape):
    lane = lax.broadcasted_iota(jnp.int32, shape, len(shape) - 1)
    out = jnp.broadcast_to(vals[-1], shape)
    for r in range(len(vals) - 2, -1, -1):
        out = jnp.where(lane < (r + 1) * width, vals[r], out)
    return out


def _ssd_row_decay(dt_row_ref, bias_row_ref, aneg_row_ref):
    dt_r = _softplus(dt_row_ref[...] + bias_row_ref[...])
    a_r = dt_r * aneg_row_ref[...]
    cs_r = _f32_dot(a_r, _upper_ones(a_r.shape[1]))
    return dt_r, a_r, cs_r


def _ssd_bwd_state_body(bt_ref, x_ref, dt_row_ref, bias_row_ref, aneg_row_ref, s0_ref,
                        s_after_ref, s_fin_ref, s_scr, *, nc):
    k = pl.program_id(2)
    nh = SSD_HEADS_PER_GROUP

    @pl.when(k == 0)
    def _():
        s_scr[...] = s0_ref[...]

    s_after_ref[...] = s_scr[...]
    dt_r, a_r, cs_r = _ssd_row_decay(dt_row_ref, bias_row_ref, aneg_row_ref)
    t = a_r.shape[1]
    ecs_r = cs_r - a_r
    bt = bt_ref[...].astype(jnp.float32)
    xb = x_ref[...].astype(jnp.bfloat16)
    upd = []
    tot = []
    for r in range(nh):
        w_row = jnp.exp(ecs_r[nh + r:nh + r + 1, :]) * dt_r[nh + r:nh + r + 1, :]
        upd.append(_bf16_dot(bt * w_row, xb))
        tot.append(jnp.exp(cs_r[nh + r:nh + r + 1, t - 1:t]))
    shape = s_scr.shape
    s_new = _head_lane_select(tot, SSD_HEAD_DIM, shape) * s_scr[...] + _head_lane_select(upd, SSD_HEAD_DIM, shape)
    s_scr[...] = s_new

    @pl.when(k == nc - 1)
    def _():
        s_fin_ref[...] = s_new


def _ssd_bwd_state(bt, xs, dt_row, bias_row, aneg_row, s0, *, bsz, seq):
    nc = seq // SCAN_T
    g = SSD_GROUPS
    rev = lambda k: nc - 1 - k
    return pl.pallas_call(
        functools.partial(_ssd_bwd_state_body, nc=nc),
        grid=(bsz, g, nc),
        in_specs=[pl.BlockSpec((None, SSD_STATE, SCAN_T), lambda b, gi, k: (b, gi, rev(k))),
                  pl.BlockSpec((SCAN_T, GROUP_W), lambda b, gi, k: (b * nc + rev(k), gi)),
                  pl.BlockSpec((None, None, N_DIR_HEADS, SCAN_T), lambda b, gi, k: (b, gi, 0, rev(k))),
                  pl.BlockSpec((None, N_DIR_HEADS, 1), lambda b, gi, k: (gi, 0, 0)),
                  pl.BlockSpec((None, N_DIR_HEADS, 1), lambda b, gi, k: (gi, 0, 0)),
                  pl.BlockSpec((None, None, SSD_STATE, GROUP_W), lambda b, gi, k: (b, gi, 0, 0))],
        out_specs=[pl.BlockSpec((None, None, None, SSD_STATE, GROUP_W), lambda b, gi, k: (b, gi, rev(k), 0, 0)),
                   pl.BlockSpec((None, None, SSD_STATE, GROUP_W), lambda b, gi, k: (b, gi, 0, 0))],
        out_shape=[jax.ShapeDtypeStruct((bsz, g, nc, SSD_STATE, GROUP_W), jnp.float32),
                   jax.ShapeDtypeStruct((bsz, g, SSD_STATE, GROUP_W), jnp.float32)],
        scratch_shapes=[pltpu.VMEM((SSD_STATE, GROUP_W), jnp.float32)],
        compiler_params=_cparams(("parallel", "parallel", "arbitrary")),
        name="ssd_bwd_state")(bt, xs, dt_row, bias_row, aneg_row, s0)


def _ssd_fwd_body(c_ref, bt_ref, x_ref, z_ref, dt_row_ref, bias_row_ref, aneg_row_ref, dskip_ref, nw_ref,
                  s0_ref, s_after_ref, o_ref, s_fin_ref, s_scr, *, nc):
    k = pl.program_id(2)
    nh = SSD_HEADS_PER_GROUP

    @pl.when(k == 0)
    def _():
        s_scr[...] = s0_ref[...]

    dt_r, a_r, cs_r = _ssd_row_decay(dt_row_ref, bias_row_ref, aneg_row_ref)
    t = a_r.shape[1]
    ecs_r = cs_r - a_r
    n_vec = 2 * N_DIR_HEADS
    cols = jnp.concatenate([cs_r, ecs_r, jnp.zeros((LANE - n_vec, t), jnp.float32)], axis=0).T
    cs_c = cols[:, :N_DIR_HEADS]
    ecs_c = cols[:, N_DIR_HEADS:n_vec]

    cq = c_ref[...]
    bt = bt_ref[...]
    x = x_ref[...]
    xb = x.astype(jnp.bfloat16)
    scores = jnp.dot(cq, bt, preferred_element_type=jnp.float32)
    row = lax.broadcasted_iota(jnp.int32, (t, t), 0)
    col = lax.broadcasted_iota(jnp.int32, (t, t), 1)
    neg_inf = jnp.float32(-jnp.inf)
    btf = bt.astype(jnp.float32)

    intra, e_f, e_b, upd, tot = [], [], [], [], []
    for r in range(nh):
        fr, br = r, nh + r
        seg_f = jnp.where(row >= col, cs_c[:, fr:fr + 1] - cs_r[fr:fr + 1, :], neg_inf)
        seg_b = jnp.where(col >= row, ecs_r[br:br + 1, :] - ecs_c[:, br:br + 1], neg_inf)
        wgt = jnp.exp(seg_f) * dt_r[fr:fr + 1, :] + jnp.exp(seg_b) * dt_r[br:br + 1, :]
        intra.append(_bf16_dot(scores * wgt, xb))
        e_f.append(jnp.exp(cs_c[:, fr:fr + 1]))
        e_b.append(jnp.exp(cs_r[br:br + 1, t - 1:t] - ecs_c[:, br:br + 1]))
        tot_f = cs_r[fr:fr + 1, t - 1:t]
        w_row = jnp.exp(tot_f - cs_r[fr:fr + 1, :]) * dt_r[fr:fr + 1, :]
        upd.append(_bf16_dot(btf * w_row, xb))
        tot.append(jnp.exp(tot_f))

    shape = x.shape
    s_f = s_scr[...]
    y = _head_lane_select(intra, SSD_HEAD_DIM, shape)
    y = y + _head_lane_select(e_f, SSD_HEAD_DIM, shape) * _bf16_dot(cq, s_f)
    y = y + _head_lane_select(e_b, SSD_HEAD_DIM, shape) * _bf16_dot(cq, s_after_ref[...])
    y = (y + dskip_ref[...] * x) * _silu(z_ref[...])
    y = y * lax.rsqrt(jnp.mean(y * y, axis=-1, keepdims=True) + NORM_EPS) * nw_ref[...]
    o_ref[...] = y.astype(o_ref.dtype)

    s_new = (_head_lane_select(tot, SSD_HEAD_DIM, s_f.shape) * s_f
             + _head_lane_select(upd, SSD_HEAD_DIM, s_f.shape))
    s_scr[...] = s_new

    @pl.when(k == nc - 1)
    def _():
        s_fin_ref[...] = s_new


def _ssd_fwd(cm, bt, xs, proj, dt_row, bias_row, aneg_row, dskip, norm_w, s0, s_after, *, bsz, seq):
    nc = seq // SCAN_T
    g = SSD_GROUPS
    zcol = COL_SSD_Z * LANE // GROUP_W
    vec_row = pl.BlockSpec((None, N_DIR_HEADS, 1), lambda b, gi, k: (gi, 0, 0))
    return pl.pallas_call(
        functools.partial(_ssd_fwd_body, nc=nc),
        grid=(bsz, g, nc),
        in_specs=[pl.BlockSpec((SCAN_T, SSD_STATE), lambda b, gi, k: (b * nc + k, gi)),
                  pl.BlockSpec((None, SSD_STATE, SCAN_T), lambda b, gi, k: (b, gi, k)),
                  pl.BlockSpec((SCAN_T, GROUP_W), lambda b, gi, k: (b * nc + k, gi)),
                  pl.BlockSpec((SCAN_T, GROUP_W), lambda b, gi, k: (b * nc + k, zcol + gi)),
                  pl.BlockSpec((None, None, N_DIR_HEADS, SCAN_T), lambda b, gi, k: (b, gi, 0, k)),
                  vec_row, vec_row,
                  pl.BlockSpec((1, GROUP_W), lambda b, gi, k: (0, gi)),
                  pl.BlockSpec((1, GROUP_W), lambda b, gi, k: (0, gi)),
                  pl.BlockSpec((None, None, SSD_STATE, GROUP_W), lambda b, gi, k: (b, gi, 0, 0)),
                  pl.BlockSpec((None, None, None, SSD_STATE, GROUP_W), lambda b, gi, k: (b, gi, k, 0, 0))],
        out_specs=[pl.BlockSpec((SCAN_T, GROUP_W), lambda b, gi, k: (b * nc + k, gi)),
                   pl.BlockSpec((None, None, SSD_STATE, GROUP_W), lambda b, gi, k: (b, gi, 0, 0))],
        out_shape=[jax.ShapeDtypeStruct((bsz * seq, SSD_WIDTH), jnp.bfloat16),
                   jax.ShapeDtypeStruct((bsz, g, SSD_STATE, GROUP_W), jnp.float32)],
        scratch_shapes=[pltpu.VMEM((SSD_STATE, GROUP_W), jnp.float32)],
        compiler_params=_cparams(("parallel", "parallel", "arbitrary")), name="ssd_fwd")(
            cm, bt, xs, proj, dt_row, bias_row, aneg_row, dskip, norm_w, s0, s_after)


def _ssd_group(proj, proj_dt, proj_c, proj_c_dt, lw, *, bsz, seq, n_ctx, need_ctx_out):
    g, nh = SSD_GROUPS, SSD_HEADS_PER_GROUP
    conv_w = lw['ssd_conv_w']
    conv_b = lw['ssd_conv_b'][None, :]

    def per_group(v):
        return v.astype(jnp.float32).reshape(2, g, nh).transpose(1, 0, 2).reshape(g, 2 * nh)

    bias = per_group(lw['ssd_dt_bias'])
    aneg = per_group(-jnp.exp(lw['ssd_a_log'].astype(jnp.float32)))
    bias_row, aneg_row = bias[:, :, None], aneg[:, :, None]
    dskip = jnp.repeat(lw['ssd_d'].astype(jnp.float32), SSD_HEAD_DIM)[None, :]
    norm_w = lw['ssd_norm_w'][None, :]

    def prepare(p, p_dt, n):
        conv = functools.partial(_ssd_conv, p, conv_w, conv_b, bsz=bsz, seq=n)
        xs = conv(col0=COL_SSD_X, ncols=SSD_WIDTH // LANE, conv_col0=0, transpose=False, out_dtype=jnp.float32)
        bt = conv(col0=COL_SSD_B, ncols=g, conv_col0=SSD_WIDTH // LANE, transpose=True, out_dtype=jnp.bfloat16)
        cm = conv(col0=COL_SSD_C, ncols=g, conv_col0=SSD_WIDTH // LANE + g, transpose=False,
                  out_dtype=jnp.bfloat16)
        dt_row = p_dt[:, :DT_WIDTH].reshape(bsz, n, 2, g, nh).transpose(0, 3, 2, 4, 1).reshape(bsz, g, 2 * nh, n)
        return xs, bt, cm, dt_row

    zeros = jnp.zeros((bsz, g, SSD_STATE, GROUP_W), jnp.float32)
    xs_c, bt_c, cm_c, dtr_c = prepare(proj_c, proj_c_dt, n_ctx)
    xs, bt, cm, dt_row = prepare(proj, proj_dt, seq)
    sa_c, sb0 = _ssd_bwd_state(bt_c, xs_c, dtr_c, bias_row, aneg_row, zeros, bsz=bsz, seq=n_ctx)
    fwd = functools.partial(_ssd_fwd, bias_row=bias_row, aneg_row=aneg_row, dskip=dskip, norm_w=norm_w, bsz=bsz)
    y_c, sf0 = fwd(cm_c, bt_c, xs_c, proj_c, dtr_c, s0=zeros, s_after=sa_c, seq=n_ctx)
    sa, _ = _ssd_bwd_state(bt, xs, dt_row, bias_row, aneg_row, sb0, bsz=bsz, seq=seq)
    y, _ = fwd(cm, bt, xs, proj, dt_row, s0=sf0, s_after=sa, seq=seq)
    return y, (y_c if need_ctx_out else None)


def _ret_prep_body(q_ref, k_ref, cos_ref, sin_ref, qo_ref, kto_ref, *, use_rope):
    q = q_ref[...]
    kk = k_ref[...]
    if use_rope:
        q = _rope(q, cos_ref[...], sin_ref[...], RET_HEAD_DIM // 4)
        kk = _rope(kk, cos_ref[...], sin_ref[...], RET_HEAD_DIM // 4)
    kk = kk * (RET_HEAD_DIM ** -0.5)
    qo_ref[...] = q.astype(qo_ref.dtype)
    kto_ref[...] = kk.T.astype(kto_ref.dtype)


def _ret_prep(proj, cos, sin, *, bsz, seq, use_rope):
    tr = min(512, seq)
    nr = seq // tr
    return pl.pallas_call(
        functools.partial(_ret_prep_body, use_rope=use_rope),
        grid=(bsz, RET_HEADS, nr),
        in_specs=[pl.BlockSpec((tr, LANE), lambda b, h, i: (b * nr + i, COL_RET_Q + h)),
                  pl.BlockSpec((tr, LANE), lambda b, h, i: (b * nr + i, COL_RET_K + h)),
                  pl.BlockSpec((tr, LANE), lambda b, h, i: (i, 0)),
                  pl.BlockSpec((tr, LANE), lambda b, h, i: (i, 0))],
        out_specs=[pl.BlockSpec((tr, LANE), lambda b, h, i: (b * nr + i, h)),
                   pl.BlockSpec((None, LANE, tr), lambda b, h, i: (b, h, i))],
        out_shape=[jax.ShapeDtypeStruct((bsz * seq, RET_WIDTH), jnp.bfloat16),
                   jax.ShapeDtypeStruct((bsz, RET_WIDTH, seq), jnp.bfloat16)],
        compiler_params=_cparams(("parallel", "parallel", "parallel")), name="ret_prep")(proj, proj, cos, sin)


def _ret_bwd_state_body(lg_ref, kt_ref, v_ref, s0_ref, s_after_ref, s_fin_ref, s_scr, *, nc):
    h = pl.program_id(1)
    k = pl.program_id(2)

    @pl.when(k == 0)
    def _():
        s_scr[...] = s0_ref[...]

    s_after_ref[...] = s_scr[...]
    lgb = jnp.full((1, 1), lg_ref[1, h], jnp.float32)
    t = kt_ref.shape[1]
    j = lax.broadcasted_iota(jnp.int32, (1, t), 1).astype(jnp.float32)
    kt = kt_ref[...].astype(jnp.float32) * jnp.exp(j * lgb)
    s_new = jnp.exp(t * lgb) * s_scr[...] + _bf16_dot(kt, v_ref[...])
    s_scr[...] = s_new

    @pl.when(k == nc - 1)
    def _():
        s_fin_ref[...] = s_new


def _ret_bwd_state(lg, kt, proj, s0, *, bsz, seq):
    nc = seq // SCAN_T
    rev = lambda k: nc - 1 - k
    grid_spec = pltpu.PrefetchScalarGridSpec(
        num_scalar_prefetch=0, grid=(bsz, RET_HEADS, nc),
        in_specs=[pl.BlockSpec(memory_space=pltpu.SMEM),
                  pl.BlockSpec((None, LANE, SCAN_T), lambda b, h, k: (b, h, rev(k))),
                  pl.BlockSpec((SCAN_T, LANE), lambda b, h, k: (b * nc + rev(k), COL_RET_V + h)),
                  pl.BlockSpec((None, None, LANE, LANE), lambda b, h, k: (b, h, 0, 0))],
        out_specs=[pl.BlockSpec((None, None, None, LANE, LANE), lambda b, h, k: (b, h, rev(k), 0, 0)),
                   pl.BlockSpec((None, None, LANE, LANE), lambda b, h, k: (b, h, 0, 0))],
        scratch_shapes=[pltpu.VMEM((LANE, LANE), jnp.float32)])
    return pl.pallas_call(
        functools.partial(_ret_bwd_state_body, nc=nc), grid_spec=grid_spec,
        out_shape=[jax.ShapeDtypeStruct((bsz, RET_HEADS, nc, LANE, LANE), jnp.float32),
                   jax.ShapeDtypeStruct((bsz, RET_HEADS, LANE, LANE), jnp.float32)],
        compiler_params=_cparams(("parallel", "parallel", "arbitrary")), name="ret_bwd_state")(lg, kt, proj, s0)


def _ret_fwd_body(lg_ref, q_ref, kt_ref, v_ref, g_ref, nw_ref, s0_ref, s_after_ref,
                  o_ref, s_fin_ref, s_scr, d_scr, *, nc):
    h = pl.program_id(1)
    k = pl.program_id(2)
    lgf = jnp.full((1, 1), lg_ref[0, h], jnp.float32)
    lgb = jnp.full((1, 1), lg_ref[1, h], jnp.float32)
    t = q_ref.shape[0]

    @pl.when(k == 0)
    def _():
        s_scr[...] = s0_ref[...]
        row = lax.broadcasted_iota(jnp.int32, (t, t), 0)
        col = lax.broadcasted_iota(jnp.int32, (t, t), 1)
        dist = (row - col).astype(jnp.float32)
        neg_inf = jnp.float32(-jnp.inf)
        d_scr[...] = (jnp.exp(jnp.where(row >= col, dist * lgf, neg_inf))
                      + jnp.exp(jnp.where(col >= row, -dist * lgb, neg_inf)))

    q = q_ref[...]
    kt = kt_ref[...]
    v = v_ref[...].astype(jnp.bfloat16)
    scores = jnp.dot(q, kt, preferred_element_type=jnp.float32)
    i = lax.broadcasted_iota(jnp.int32, (t, 1), 0).astype(jnp.float32)
    j = lax.broadcasted_iota(jnp.int32, (1, t), 1).astype(jnp.float32)
    s_f = s_scr[...]
    y = _bf16_dot(scores * d_scr[...], v)
    y = y + jnp.exp((i + 1.0) * lgf) * _bf16_dot(q, s_f)
    y = y + jnp.exp((t - i) * lgb) * _bf16_dot(q, s_after_ref[...])
    y = y * lax.rsqrt(jnp.mean(y * y, axis=-1, keepdims=True) + NORM_EPS) * nw_ref[...]
    o_ref[...] = (_silu(g_ref[...]) * y).astype(o_ref.dtype)

    ktw = kt.astype(jnp.float32) * jnp.exp((t - 1.0 - j) * lgf)
    s_new = jnp.exp(t * lgf) * s_f + _bf16_dot(ktw, v)
    s_scr[...] = s_new

    @pl.when(k == nc - 1)
    def _():
        s_fin_ref[...] = s_new


def _ret_fwd(lg, q, kt, proj, norm_w, s0, s_after, *, bsz, seq):
    nc = seq // SCAN_T
    grid_spec = pltpu.PrefetchScalarGridSpec(
        num_scalar_prefetch=0, grid=(bsz, RET_HEADS, nc),
        in_specs=[pl.BlockSpec(memory_space=pltpu.SMEM),
                  pl.BlockSpec((SCAN_T, LANE), lambda b, h, k: (b * nc + k, h)),
                  pl.BlockSpec((None, LANE, SCAN_T), lambda b, h, k: (b, h, k)),
                  pl.BlockSpec((SCAN_T, LANE), lambda b, h, k: (b * nc + k, COL_RET_V + h)),
                  pl.BlockSpec((SCAN_T, LANE), lambda b, h, k: (b * nc + k, COL_RET_G + h)),
                  pl.BlockSpec((1, LANE), lambda b, h, k: (0, h)),
                  pl.BlockSpec((None, None, LANE, LANE), lambda b, h, k: (b, h, 0, 0)),
                  pl.BlockSpec((None, None, None, LANE, LANE), lambda b, h, k: (b, h, k, 0, 0))],
        out_specs=[pl.BlockSpec((SCAN_T, LANE), lambda b, h, k: (b * nc + k, h)),
                   pl.BlockSpec((None, None, LANE, LANE), lambda b, h, k: (b, h, 0, 0))],
        scratch_shapes=[pltpu.VMEM((LANE, LANE), jnp.float32), pltpu.VMEM((SCAN_T, SCAN_T), jnp.float32)])
    return pl.pallas_call(
        functools.partial(_ret_fwd_body, nc=nc), grid_spec=grid_spec,
        out_shape=[jax.ShapeDtypeStruct((bsz * seq, RET_WIDTH), jnp.bfloat16),
                   jax.ShapeDtypeStruct((bsz, RET_HEADS, LANE, LANE), jnp.float32)],
        compiler_params=_cparams(("parallel", "parallel", "arbitrary")), name="ret_fwd")(
            lg, q, kt, proj, proj, norm_w, s0, s_after)


def _ret_group(proj, proj_c, lw, rope, *, bsz, seq, n_ctx, need_ctx_out):
    cos, sin = rope
    lg = jax.nn.log_sigmoid(lw['ret_decay_logit'].astype(jnp.float32))
    norm_w = lw['ret_norm_w'][None, :]
    zeros = jnp.zeros((bsz, RET_HEADS, LANE, LANE), jnp.float32)
    q_c, kt_c = _ret_prep(proj_c, cos, sin, bsz=bsz, seq=n_ctx, use_rope=False)
    q, kt = _ret_prep(proj, cos, sin, bsz=bsz, seq=seq, use_rope=True)
    sa_c, sb0 = _ret_bwd_state(lg, kt_c, proj_c, zeros, bsz=bsz, seq=n_ctx)
    y_c, sf0 = _ret_fwd(lg, q_c, kt_c, proj_c, norm_w, zeros, sa_c, bsz=bsz, seq=n_ctx)
    sa, _ = _ret_bwd_state(lg, kt, proj, sb0, bsz=bsz, seq=seq)
    y, _ = _ret_fwd(lg, q, kt, proj, norm_w, sf0, sa, bsz=bsz, seq=seq)
    return y, (y_c if need_ctx_out else None)


def _diff_prep_body(q_ref, k_ref, v_ref, cos_ref, sin_ref, qt1_ref, qt2_ref, ko_ref, vto_ref, *, use_rope):
    q = q_ref[...]
    kk = k_ref[...]
    if use_rope:
        q = _rope(q, cos_ref[...], sin_ref[...], DIFF_QK_DIM // 4)
        kk = _rope(kk, cos_ref[...], sin_ref[...], DIFF_QK_DIM // 4)
    qt = (q * (DIFF_QK_DIM ** -0.5)).T
    sub = lax.broadcasted_iota(jnp.int32, qt.shape, 0)
    qt1_ref[...] = jnp.where(sub < DIFF_QK_DIM, qt, 0.0).astype(qt1_ref.dtype)
    qt2_ref[...] = jnp.where(sub >= DIFF_QK_DIM, qt, 0.0).astype(qt2_ref.dtype)
    ko_ref[...] = kk.astype(ko_ref.dtype)
    vto_ref[...] = v_ref[...].T.astype(vto_ref.dtype)


def _diff_prep(proj, cos, sin, *, bsz, seq, use_rope):
    tr = KV_CHUNK
    nr = seq // tr
    t_spec = pl.BlockSpec((None, LANE, tr), lambda b, h, i: (b, h, i))
    t_shape = jax.ShapeDtypeStruct((bsz, DIFF_WIDTH, seq), jnp.bfloat16)
    return pl.pallas_call(
        functools.partial(_diff_prep_body, use_rope=use_rope),
        grid=(bsz, DIFF_HEADS, nr),
        in_specs=[pl.BlockSpec((tr, LANE), lambda b, h, i: (b * nr + i, COL_DIFF_Q + h)),
                  pl.BlockSpec((tr, LANE), lambda b, h, i: (b * nr + i, COL_DIFF_K + h)),
                  pl.BlockSpec((tr, LANE), lambda b, h, i: (b * nr + i, COL_DIFF_V + h)),
                  pl.BlockSpec((tr, LANE), lambda b, h, i: (i, 0)),
                  pl.BlockSpec((tr, LANE), lambda b, h, i: (i, 0))],
        out_specs=[t_spec, t_spec,
                   pl.BlockSpec((None, tr, LANE), lambda b, h, i: (b, i, h)),
                   pl.BlockSpec((None, None, LANE, tr), lambda b, h, i: (b, i, h, 0))],
        out_shape=[t_shape, t_shape,
                   jax.ShapeDtypeStruct((bsz, seq, DIFF_WIDTH), jnp.bfloat16),
                   jax.ShapeDtypeStruct((bsz, nr, DIFF_WIDTH, tr), jnp.bfloat16)],
        compiler_params=_cparams(("parallel", "parallel", "parallel")), name="diff_prep")(
            proj, proj, proj, cos, sin)


def _diff_attn_body(lam_ref, qt1_ref, qt2_ref, k_ref, vt_ref, nw_ref, o_ref, acc1, acc2, *, chunks, out_scale):
    tk = chunks * KV_CHUNK
    n_steps = k_ref.shape[0] // tk
    tq = qt1_ref.shape[1]
    qts = (qt1_ref[...], qt2_ref[...])
    accs = (acc1, acc2)
    for acc in accs:
        acc[...] = jnp.zeros_like(acc)

    def kv_step(j, carry):
        start = pl.multiple_of(j * tk, tk)
        kk = k_ref[pl.ds(start, tk), :]
        new = []
        for (m, l), qt, acc in zip(carry, qts, accs):
            s = jnp.dot(kk, qt, preferred_element_type=jnp.float32)
            m_new = jnp.maximum(m, jnp.max(s, axis=0, keepdims=True))
            alpha = jnp.exp(m - m_new)
            p = jnp.exp(s - m_new)
            l_new = alpha * l + jnp.sum(p, axis=0, keepdims=True)
            pb = p.astype(jnp.bfloat16)
            pv = jnp.dot(vt_ref[j * chunks], pb[:KV_CHUNK], preferred_element_type=jnp.float32)
            for c in range(1, chunks):
                pv = pv + jnp.dot(vt_ref[j * chunks + c], pb[c * KV_CHUNK:(c + 1) * KV_CHUNK],
                                  preferred_element_type=jnp.float32)
            acc[...] = alpha * acc[...] + pv
            new.append((m_new, l_new))
        return tuple(new)

    init = tuple((jnp.full((1, tq), -jnp.inf, jnp.float32), jnp.zeros((1, tq), jnp.float32)) for _ in range(2))
    (m1, l1), (m2, l2) = lax.fori_loop(0, n_steps, kv_step, init)
    o = acc1[...] / l1 - lam_ref[0] * (acc2[...] / l2)
    o = o * lax.rsqrt(jnp.mean(o * o, axis=0, keepdims=True) + NORM_EPS)
    o_ref[...] = (o.T * nw_ref[...] * out_scale).astype(o_ref.dtype)


def _diff_attn(lam, qt1, qt2, k_all, vt_all, norm_w, *, bsz, n_q, lambda_init, tq):
    nq = n_q // tq
    kv_len = k_all.shape[1]
    n_chunks = kv_len // KV_CHUNK
    chunks = 3 if n_chunks % 3 == 0 else 1
    grid_spec = pltpu.PrefetchScalarGridSpec(
        num_scalar_prefetch=0, grid=(bsz, DIFF_HEADS, nq),
        in_specs=[pl.BlockSpec(memory_space=pltpu.SMEM),
                  pl.BlockSpec((None, LANE, tq), lambda b, h, i: (b, h, i)),
                  pl.BlockSpec((None, LANE, tq), lambda b, h, i: (b, h, i)),
                  pl.BlockSpec((None, kv_len, LANE), lambda b, h, i: (b, 0, h)),
                  pl.BlockSpec((None, n_chunks, LANE, KV_CHUNK), lambda b, h, i: (b, 0, h, 0)),
                  pl.BlockSpec((1, LANE), lambda b, h, i: (0, 0))],
        out_specs=pl.BlockSpec((tq, LANE), lambda b, h, i: (b * nq + i, h)),
        scratch_shapes=[pltpu.VMEM((LANE, tq), jnp.float32), pltpu.VMEM((LANE, tq), jnp.float32)])
    return pl.pallas_call(
        functools.partial(_diff_attn_body, chunks=chunks, out_scale=1.0 - lambda_init), grid_spec=grid_spec,
        out_shape=jax.ShapeDtypeStruct((bsz * n_q, DIFF_WIDTH), jnp.bfloat16),
        compiler_params=_cparams(("parallel", "parallel", "arbitrary")), name="diff_attn")(
            lam, qt1, qt2, k_all, vt_all, norm_w)


def _diff_group(proj, proj_c, lw, rope, *, bsz, seq, n_ctx, lambda_init, need_ctx_out):
    cos, sin = rope
    lv = lw['diff_lambda'].astype(jnp.float32)
    lam = (jnp.exp(jnp.sum(lv[0] * lv[1])) - jnp.exp(jnp.sum(lv[2] * lv[3])) + lambda_init).reshape(1)
    norm_w = lw['diff_norm_w'][None, :]
    qt1, qt2, k_l, vt_l = _diff_prep(proj, cos, sin, bsz=bsz, seq=seq, use_rope=True)
    qt1c, qt2c, k_c, vt_c = _diff_prep(proj_c, cos, sin, bsz=bsz, seq=n_ctx, use_rope=False)
    k_all = jnp.concatenate([k_l, k_c], axis=1)
    vt_all = jnp.concatenate([vt_l, vt_c], axis=1)
    y = _diff_attn(lam, qt1, qt2, k_all, vt_all, norm_w, bsz=bsz, n_q=seq, lambda_init=lambda_init,
                   tq=min(512, seq))
    y_c = None
    if need_ctx_out:
        y_c = _diff_attn(lam, qt1c, qt2c, k_c, vt_c, norm_w, bsz=bsz, n_q=n_ctx, lambda_init=lambda_init,
                         tq=n_ctx)
    return y, y_c


def _mixers(proj, proj_dt, proj_c, proj_c_dt, lw, *, bsz, seq, n_ctx, lambda_init, need_ctx_out):
    rope_ret = _rope_tables(seq, RET_HEAD_DIM, 1)
    rope_diff = _rope_tables(seq, DIFF_QK_DIM, 2)
    r_l, r_c = _ret_group(proj, proj_c, lw, rope_ret, bsz=bsz, seq=seq, n_ctx=n_ctx, need_ctx_out=need_ctx_out)
    s_l, s_c = _ssd_group(proj, proj_dt, proj_c, proj_c_dt, lw, bsz=bsz, seq=seq, n_ctx=n_ctx,
                          need_ctx_out=need_ctx_out)
    d_l, d_c = _diff_group(proj, proj_c, lw, rope_diff, bsz=bsz, seq=seq, n_ctx=n_ctx,
                           lambda_init=lambda_init, need_ctx_out=need_ctx_out)
    mix = jnp.concatenate([r_l, s_l, d_l], axis=-1)
    mix_c = jnp.concatenate([r_c, s_c, d_c], axis=-1) if need_ctx_out else None
    return mix, mix_c


def _gather_rows_body(idx_ref, src_hbm, o_ref, sem, *, tm):
    base = pl.program_id(0) * tm

    def row_copy(r):
        return pltpu.make_async_copy(src_hbm.at[pl.ds(idx_ref[base + r], 1)], o_ref.at[pl.ds(r, 1)], sem)

    def start(r, carry):
        row_copy(r).start()
        return carry

    def wait(r, carry):
        row_copy(r).wait()
        return carry

    lax.fori_loop(0, tm, start, 0)
    lax.fori_loop(0, tm, wait, 0)


def _gather_rows(idx, src, *, tm):
    n_rows = idx.shape[0]
    d = src.shape[1]
    grid_spec = pltpu.PrefetchScalarGridSpec(
        num_scalar_prefetch=1, grid=(n_rows // tm,),
        in_specs=[pl.BlockSpec(memory_space=pl.ANY)],
        out_specs=pl.BlockSpec((tm, d), lambda i, idx_ref: (i, 0)),
        scratch_shapes=[pltpu.SemaphoreType.DMA(())])
    return pl.pallas_call(
        functools.partial(_gather_rows_body, tm=tm), grid_spec=grid_spec,
        out_shape=jax.ShapeDtypeStruct((n_rows, d), src.dtype),
        compiler_params=_cparams(("arbitrary",)), name="moe_gather_rows")(idx, src)


def _moe_combine_body(pos_ref, y_hbm, g_ref, res_ref, gate_ref, o_ref, buf, sem, *, tm, n_steps):
    i = pl.program_id(0)

    def row_copy(step, slot, r, kk):
        src_row = pos_ref[(step * tm + r) * TOP_K + kk]
        return pltpu.make_async_copy(y_hbm.at[pl.ds(src_row, 1)], buf.at[slot, kk, pl.ds(r, 1)], sem.at[slot])

    def start_block(step, slot):
        def body(r, carry):
            for kk in range(TOP_K):
                row_copy(step, slot, r, kk).start()
            return carry
        lax.fori_loop(0, tm, body, 0)

    def wait_block(step, slot):
        def body(r, carry):
            for kk in range(TOP_K):
                row_copy(step, slot, r, kk).wait()
            return carry
        lax.fori_loop(0, tm, body, 0)

    slot = i % 2

    @pl.when(i == 0)
    def _():
        start_block(0, 0)

    @pl.when(i + 1 < n_steps)
    def _():
        start_block(i + 1, 1 - slot)

    wait_block(i, slot)
    g = g_ref[...]
    f = g[:, 0:1] * buf[slot, 0]
    for kk in range(1, TOP_K):
        f = f + g[:, kk:kk + 1] * buf[slot, kk]
    o_ref[...] = res_ref[...] + gate_ref[...] * f


def _moe_combine(pos, y_rows, gates, res, gate_vec, rows_per_gate, *, tm=128):
    n_tok, d = res.shape
    n_steps = n_tok // tm
    grid_spec = pltpu.PrefetchScalarGridSpec(
        num_scalar_prefetch=1, grid=(n_steps,),
        in_specs=[pl.BlockSpec(memory_space=pl.ANY),
                  pl.BlockSpec((tm, TOP_K), lambda i, p: (i, 0)),
                  pl.BlockSpec((tm, d), lambda i, p: (i, 0)),
                  pl.BlockSpec((None, 1, d), lambda i, p: ((i * tm) // rows_per_gate, 0, 0))],
        out_specs=pl.BlockSpec((tm, d), lambda i, p: (i, 0)),
        scratch_shapes=[pltpu.VMEM((2, TOP_K, tm, d), jnp.float32), pltpu.SemaphoreType.DMA((2,))])
    return pl.pallas_call(
        functools.partial(_moe_combine_body, tm=tm, n_steps=n_steps), grid_spec=grid_spec,
        out_shape=jax.ShapeDtypeStruct((n_tok, d), jnp.float32),
        compiler_params=_cparams(("arbitrary",)), name="moe_combine")(pos, y_rows, gates, res, gate_vec)


def _moe(h, router_w, w_gate, w_up, w_down, res, gate_vec, rows_per_gate):
    n_tok, d = h.shape
    n_exp = N_EXPERTS
    n_assign = n_tok * TOP_K
    logits = _matmul(h, [router_w], tm=512, tn=LANE, tk=d, out_dtype=jnp.float32, name="moe_router")[:, :n_exp]
    top_logits, top_idx = lax.top_k(logits, TOP_K)
    gates = jax.nn.softmax(top_logits, axis=-1)
    expert = top_idx.reshape(n_assign)
    order = jnp.argsort(expert)
    expert_s = expert[order]
    token_s = order // TOP_K
    counts = jnp.bincount(expert, length=n_exp)
    starts = jnp.cumsum(counts) - counts
    padded = (counts + MOE_BLOCK - 1) // MOE_BLOCK * MOE_BLOCK
    pad_ends = jnp.cumsum(padded)
    pad_starts = pad_ends - padded
    dest = pad_starts[expert_s] + jnp.arange(n_assign) - starts[expert_s]
    n_blocks = -(-n_assign // MOE_BLOCK) + n_exp
    src_tok = jnp.zeros((n_blocks * MOE_BLOCK,), jnp.int32).at[dest].set(token_s.astype(jnp.int32))
    rows = _gather_rows(src_tok, h, tm=MOE_BLOCK)
    block_expert = jnp.minimum(
        jnp.searchsorted(pad_ends, jnp.arange(n_blocks) * MOE_BLOCK, side='right'), n_exp - 1
    ).astype(jnp.int32)
    t = _matmul(rows, [w_gate, w_up], tm=MOE_BLOCK, tn=512, tk=d, out_dtype=jnp.bfloat16,
                block_expert=block_expert, n_major=True, name="moe_gate_up")
    y_rows = _matmul(t, [w_down], tm=MOE_BLOCK, tn=1024, tk=w_down.shape[1], out_dtype=jnp.float32,
                     block_expert=block_expert, n_major=True, name="moe_down")
    pos = jnp.zeros((n_assign,), jnp.int32).at[order].set(dest.astype(jnp.int32))
    return _moe_combine(pos, y_rows, gates, res, gate_vec, rows_per_gate)


def kernel(x, c, ctx, c_ctx, ada_w, ada_b, norm_mix_w, norm_ffn_w, w_in, w_out, ret_decay_logit, ret_norm_w, ssd_conv_w, ssd_conv_b, ssd_dt_bias, ssd_a_log, ssd_d, ssd_norm_w, diff_lambda, diff_norm_w, dense_w_gate, dense_w_up, dense_w_down, moe_router, moe_w_gate, moe_w_up, moe_w_down, final_norm_w):
    bsz, seq, d = x.shape
    n_ctx = ctx.shape[1]
    assert n_ctx % SCAN_T == 0 and n_ctx % KV_CHUNK == 0 and seq % 1024 == 0
    bf16 = jnp.bfloat16

    xl = x.reshape(bsz * seq, d)
    xc = ctx.reshape(bsz * n_ctx, d)
    cond = jnp.concatenate([c, c_ctx[None, :], jnp.zeros((8 - bsz - 1, d), jnp.float32)], axis=0)

    for layer in range(DEPTH):
        last = layer == DEPTH - 1
        lambda_init = 0.8 - 0.6 * math.exp(-0.3 * layer)
        lw = {'ret_decay_logit': ret_decay_logit[layer], 'ret_norm_w': ret_norm_w[layer],
              'ssd_conv_w': ssd_conv_w[layer], 'ssd_conv_b': ssd_conv_b[layer],
              'ssd_dt_bias': ssd_dt_bias[layer], 'ssd_a_log': ssd_a_log[layer], 'ssd_d': ssd_d[layer],
              'ssd_norm_w': ssd_norm_w[layer], 'diff_lambda': diff_lambda[layer],
              'diff_norm_w': diff_norm_w[layer]}

        mod = _ada_modulation(cond, ada_w[layer], ada_b[layer][None, :])
        mod_l = mod[:bsz].reshape(bsz, 1, 6, d)
        mod_c = mod[bsz:bsz + 1].reshape(1, 1, 6, d)
        shift_m, scale_m, gate_m, shift_f, scale_f, gate_f = [mod_l[:, :, i] for i in range(6)]
        shift_mc, scale_mc, gate_mc, shift_fc, scale_fc, gate_fc = [mod_c[:, :, i] for i in range(6)]

        w_in_l = w_in[layer]
        w_main = jnp.concatenate([w_in_l[:, :IN_OFFS[6]], w_in_l[:, IN_OFFS[7]:]], axis=1).astype(bf16)
        w_dt = jnp.pad(w_in_l[:, IN_OFFS[6]:IN_OFFS[7]], ((0, 0), (0, DT_PAD - DT_WIDTH))).astype(bf16)
        w_out_l = w_out[layer].astype(bf16)
        nmw = norm_mix_w[layer][None, :]
        nfw = norm_ffn_w[layer][None, :]

        h = _rmsnorm(xl, nmw, shift=shift_m, scale=scale_m, rows_per_vec=seq, out_dtype=bf16)
        hc = _rmsnorm(xc, nmw, shift=shift_mc, scale=scale_mc, rows_per_vec=bsz * n_ctx, out_dtype=bf16)
        proj = _matmul(h, [w_main], tm=1024, tn=1024, tk=d, out_dtype=jnp.float32, name="in_proj")
        proj_dt = _matmul(h, [w_dt], tm=1024, tn=DT_PAD, tk=d, out_dtype=jnp.float32, name="in_proj_dt")
        proj_c = _matmul(hc, [w_main], tm=512, tn=1024, tk=d, out_dtype=jnp.float32, name="in_proj_ctx")
        proj_c_dt = _matmul(hc, [w_dt], tm=512, tn=DT_PAD, tk=d, out_dtype=jnp.float32, name="in_proj_dt_ctx")
        mix, mix_c = _mixers(proj, proj_dt, proj_c, proj_c_dt, lw, bsz=bsz, seq=seq, n_ctx=n_ctx,
                             lambda_init=lambda_init, need_ctx_out=not last)
        xl = _matmul(mix, [w_out_l], tm=1024, tn=1024, tk=MIX_WIDTH // 2, out_dtype=jnp.float32,
                     res=xl, gate=gate_m, rows_per_gate=seq, name="out_proj")
        if not last:
            xc = _matmul(mix_c, [w_out_l], tm=512, tn=1024, tk=MIX_WIDTH // 2, out_dtype=jnp.float32,
                         res=xc, gate=gate_mc, rows_per_gate=bsz * n_ctx, name="out_proj_ctx")

        h = _rmsnorm(xl, nfw, shift=shift_f, scale=scale_f, rows_per_vec=seq,
                     out_dtype=bf16 if layer % 2 == 0 else jnp.float32)
        if not last:
            hc = _rmsnorm(xc, nfw, shift=shift_fc, scale=scale_fc, rows_per_vec=bsz * n_ctx, out_dtype=bf16)
        i = layer // 2
        if layer % 2 == 0:
            pad_f = D_FF_PAD - D_FF
            wg = jnp.pad(dense_w_gate[i], ((0, 0), (0, pad_f))).astype(bf16)
            wu = jnp.pad(dense_w_up[i], ((0, 0), (0, pad_f))).astype(bf16)
            wd = jnp.pad(dense_w_down[i], ((0, pad_f), (0, 0))).astype(bf16)
            t = _matmul(h, [wg, wu], tm=1024, tn=512, tk=d, out_dtype=bf16, name="ffn_gate_up")
            xl = _matmul(t, [wd], tm=1024, tn=1024, tk=2816, out_dtype=jnp.float32,
                         res=xl, gate=gate_f, rows_per_gate=seq, name="ffn_down")
            if not last:
                tc = _matmul(hc, [wg, wu], tm=512, tn=512, tk=d, out_dtype=bf16, name="ffn_gate_up_ctx")
                xc = _matmul(tc, [wd], tm=512, tn=1024, tk=2816, out_dtype=jnp.float32,
                             res=xc, gate=gate_fc, rows_per_gate=bsz * n_ctx, name="ffn_down_ctx")
        else:
            assert last, "expert layers that still carry a context stream are not supported"
            router = jnp.pad(moe_router[i], ((0, 0), (0, LANE - N_EXPERTS)))
            wg, wu, wd = moe_w_gate[i].astype(bf16), moe_w_up[i].astype(bf16), moe_w_down[i].astype(bf16)
            xl = _moe(h, router, wg, wu, wd, xl, gate_f, seq)

    out = _rmsnorm(xl, final_norm_w[None, :], out_dtype=jnp.float32)
    return out.reshape(bsz, seq, d)
```

```python
import functools
import math

import jax
import jax.numpy as jnp
import numpy as np
from jax import lax
from jax.experimental import pallas as pl
from jax.experimental.pallas import tpu as pltpu

D_MODEL = 4096
DEPTH = 2
GRID_W = 64

RET_WIDTH = D_MODEL // 4
RET_HEAD_DIM = 128
RET_HEADS = RET_WIDTH // RET_HEAD_DIM
SSD_WIDTH = D_MODEL // 2
SSD_HEAD_DIM = 64
SSD_HEADS = SSD_WIDTH // SSD_HEAD_DIM
SSD_GROUPS = 8
SSD_HEADS_PER_GROUP = SSD_HEADS // SSD_GROUPS
SSD_STATE = 128
SSD_CONV = 5
SSD_CONV_DIM = SSD_WIDTH + 2 * SSD_GROUPS * SSD_STATE
DIFF_WIDTH = D_MODEL // 4
DIFF_V_DIM = 128
DIFF_HEADS = DIFF_WIDTH // DIFF_V_DIM
DIFF_QK_DIM = DIFF_V_DIM // 2
MIX_WIDTH = RET_WIDTH + SSD_WIDTH + DIFF_WIDTH

IN_SIZES = (RET_WIDTH, RET_WIDTH, RET_WIDTH, RET_WIDTH,
            SSD_WIDTH, SSD_CONV_DIM, 2 * SSD_HEADS,
            2 * DIFF_HEADS * DIFF_QK_DIM, 2 * DIFF_HEADS * DIFF_QK_DIM, DIFF_WIDTH)
IN_WIDTH = sum(IN_SIZES)
IN_OFFS = tuple(int(s) for s in np.cumsum((0,) + IN_SIZES))
DT_WIDTH = 2 * SSD_HEADS
LANE = 128
DT_PAD = LANE
MAIN_WIDTH = IN_WIDTH - DT_WIDTH

COL_RET_Q, COL_RET_K, COL_RET_V, COL_RET_G = 0, 8, 16, 24
COL_SSD_Z = 32
COL_SSD_X, COL_SSD_B, COL_SSD_C = 48, 64, 72
COL_DIFF_Q, COL_DIFF_K, COL_DIFF_V = 80, 88, 96

KV_CHUNK = 256
SCAN_T = 256
GROUP_W = SSD_HEADS_PER_GROUP * SSD_HEAD_DIM
N_DIR_HEADS = 2 * SSD_HEADS_PER_GROUP

ROPE_BASE = 10000.0
LOG2_E = math.log2(math.e)
D_FF = 11008
N_EXPERTS = 8
TOP_K = 2
MOE_BLOCK = 512
NORM_EPS = 1e-6

VMEM_LIMIT = 56 * 1024 * 1024


def _cparams(sem):
    return pltpu.CompilerParams(dimension_semantics=sem, vmem_limit_bytes=VMEM_LIMIT)


def _silu(x):
    return x * jax.nn.sigmoid(x)


def _softplus(x):
    return jnp.maximum(x, 0.0) + jnp.log(1.0 + jnp.exp(-jnp.abs(x)))


def _f32_dot(a, b):
    return jnp.dot(a, b, preferred_element_type=jnp.float32, precision=lax.Precision.HIGHEST)


def _bf16_dot(a, b):
    return jnp.dot(a.astype(jnp.bfloat16), b.astype(jnp.bfloat16), preferred_element_type=jnp.float32)


def _mm_body(*refs, nk, n_rhs, has_res, grouped):
    refs = list(refs)
    if grouped:
        refs.pop(0)
    a_ref = refs.pop(0)
    b_refs = [refs.pop(0) for _ in range(n_rhs)]
    if has_res:
        res_ref = refs.pop(0)
        gate_ref = refs.pop(0)
    o_ref = refs.pop(0)
    acc_refs = refs

    def epilogue(accs):
        if n_rhs == 2:
            g, u = accs
            val = _silu(g) * u
        else:
            val = accs[0]
        if has_res:
            val = res_ref[...] + gate_ref[...] * val
        o_ref[...] = val.astype(o_ref.dtype)

    a = a_ref[...]
    if b_refs[0].dtype == jnp.float32:
        precision = lax.Precision.HIGHEST
    else:
        precision = None
        a = a.astype(b_refs[0].dtype)
    prods = [jnp.dot(a, b[...], preferred_element_type=jnp.float32, precision=precision) for b in b_refs]
    if nk == 1:
        epilogue(prods)
        return
    k = pl.program_id(2)

    @pl.when(k == 0)
    def _():
        for acc, p in zip(acc_refs, prods):
            acc[...] = p

    @pl.when(k > 0)
    def _():
        for acc, p in zip(acc_refs, prods):
            acc[...] += p

    @pl.when(k == nk - 1)
    def _():
        epilogue([acc[...] for acc in acc_refs])


def _matmul(a, bs, *, tm, tn, tk, out_dtype, res=None, gate=None, rows_per_gate=None,
            block_expert=None, n_major=False, name="matmul"):
    m, kdim = a.shape
    n = bs[0].shape[-1]
    nk = kdim // tk
    grouped = block_expert is not None
    n_rhs = len(bs)
    has_res = res is not None
    assert m % tm == 0 and n % tn == 0 and kdim % tk == 0

    if n_major:
        grid = (n // tn, m // tm, nk)
        ij = lambda g0, g1: (g1, g0)
    else:
        grid = (m // tm, n // tn, nk)
        ij = lambda g0, g1: (g0, g1)

    def a_map(g0, g1, k, *_):
        i, _j = ij(g0, g1)
        return (i, k)

    def b_map(g0, g1, k, *pref):
        i, j = ij(g0, g1)
        if grouped:
            return (pref[0][i], k, j)
        return (k, j)

    def o_map(g0, g1, k, *_):
        return ij(g0, g1)

    def gate_map(g0, g1, k, *_):
        i, j = ij(g0, g1)
        return ((i * tm) // rows_per_gate, 0, j)

    in_specs = [pl.BlockSpec((tm, tk), a_map)]
    b_block = (None, tk, tn) if grouped else (tk, tn)
    in_specs += [pl.BlockSpec(b_block, b_map) for _ in bs]
    operands = [a] + list(bs)
    if has_res:
        in_specs += [pl.BlockSpec((tm, tn), o_map), pl.BlockSpec((None, 1, tn), gate_map)]
        operands += [res, gate]
    scratch = [pltpu.VMEM((tm, tn), jnp.float32) for _ in bs] if nk > 1 else []
    body = functools.partial(_mm_body, nk=nk, n_rhs=n_rhs, has_res=has_res, grouped=grouped)
    grid_spec = pltpu.PrefetchScalarGridSpec(
        num_scalar_prefetch=1 if grouped else 0, grid=grid, in_specs=in_specs,
        out_specs=pl.BlockSpec((tm, tn), o_map), scratch_shapes=scratch)
    call = pl.pallas_call(
        body, grid_spec=grid_spec, out_shape=jax.ShapeDtypeStruct((m, n), out_dtype),
        compiler_params=_cparams(("parallel", "parallel", "arbitrary")), name=name)
    if grouped:
        return call(block_expert, *operands)
    return call(*operands)


def _w_in_repack_body(w_ref, main_ref, dt_ref):
    w = w_ref[...]
    lo, hi = IN_OFFS[6], IN_OFFS[7]
    main_ref[:, :lo] = w[:, :lo].astype(main_ref.dtype)
    main_ref[:, lo:] = w[:, hi:].astype(main_ref.dtype)
    pad = jnp.zeros((w.shape[0], DT_PAD - DT_WIDTH), w.dtype)
    dt_ref[...] = jnp.concatenate([w[:, lo:hi], pad], axis=1).astype(dt_ref.dtype)


def _w_in_repack(w_in, layer, *, tr=256):
    kdim = w_in.shape[1]
    return pl.pallas_call(
        _w_in_repack_body, grid=(kdim // tr,),
        in_specs=[pl.BlockSpec((None, tr, IN_WIDTH), lambda i: (layer, i, 0))],
        out_specs=[pl.BlockSpec((tr, MAIN_WIDTH), lambda i: (i, 0)), pl.BlockSpec((tr, DT_PAD), lambda i: (i, 0))],
        out_shape=[jax.ShapeDtypeStruct((kdim, MAIN_WIDTH), jnp.bfloat16),
                   jax.ShapeDtypeStruct((kdim, DT_PAD), jnp.bfloat16)],
        compiler_params=_cparams(("parallel",)), name="w_in_repack")(w_in)


def _ada_body(c_ref, w_ref, b_ref, o_ref):
    act = _silu(c_ref[...]).astype(jnp.bfloat16)
    o_ref[...] = jnp.dot(act, w_ref[...].astype(jnp.bfloat16),
                         preferred_element_type=jnp.float32) + b_ref[...]


def _ada_modulation(cond, w, b):
    rows, d = cond.shape
    n = w.shape[1]
    tn = 512
    return pl.pallas_call(
        _ada_body, grid=(n // tn,),
        in_specs=[pl.BlockSpec((rows, d), lambda j: (0, 0)),
                  pl.BlockSpec((d, tn), lambda j: (0, j)),
                  pl.BlockSpec((1, tn), lambda j: (0, j))],
        out_specs=pl.BlockSpec((rows, tn), lambda j: (0, j)),
        out_shape=jax.ShapeDtypeStruct((rows, n), jnp.float32),
        compiler_params=_cparams(("parallel",)), name="ada_modulation")(cond, w, b)


def _norm_body(*refs, modulated):
    if modulated:
        x_ref, w_ref, shift_ref, scale_ref, o_ref = refs
    else:
        x_ref, w_ref, o_ref = refs
    x = x_ref[...]
    y = x * lax.rsqrt(jnp.mean(x * x, axis=-1, keepdims=True) + NORM_EPS) * w_ref[...]
    if modulated:
        y = y * (1.0 + scale_ref[...]) + shift_ref[...]
    o_ref[...] = y.astype(o_ref.dtype)


def _rmsnorm(x, w, *, shift=None, scale=None, rows_per_vec=None, out_dtype, tm=256):
    m, d = x.shape
    modulated = shift is not None
    in_specs = [pl.BlockSpec((tm, d), lambda i: (i, 0)), pl.BlockSpec((1, d), lambda i: (0, 0))]
    operands = [x, w]
    if modulated:
        vec = pl.BlockSpec((None, 1, d), lambda i: ((i * tm) // rows_per_vec, 0, 0))
        in_specs += [vec, vec]
        operands += [shift, scale]
    return pl.pallas_call(
        functools.partial(_norm_body, modulated=modulated), grid=(m // tm,),
        in_specs=in_specs, out_specs=pl.BlockSpec((tm, d), lambda i: (i, 0)),
        out_shape=jax.ShapeDtypeStruct((m, d), out_dtype),
        compiler_params=_cparams(("parallel",)), name="rmsnorm")(*operands)


def _rope_tables(seq, head_dim, reps):
    rows = seq // GRID_W
    r = jnp.broadcast_to(jnp.arange(rows, dtype=jnp.float32)[:, None], (rows, GRID_W)).reshape(seq)
    col = jnp.broadcast_to(jnp.arange(GRID_W, dtype=jnp.float32)[None, :], (rows, GRID_W)).reshape(seq)
    n_freq = head_dim // 4
    inv_freq = ROPE_BASE ** (-jnp.arange(n_freq, dtype=jnp.float32) / n_freq)
    ang_r = r[:, None] * inv_freq
    ang_c = col[:, None] * inv_freq
    ang = jnp.concatenate([ang_r, ang_r, ang_c, ang_c], axis=-1)
    sign = jnp.concatenate([-jnp.ones(n_freq), jnp.ones(n_freq), -jnp.ones(n_freq), jnp.ones(n_freq)])
    return jnp.tile(jnp.cos(ang), (1, reps)), jnp.tile(jnp.sin(ang) * sign, (1, reps))


def _rope(t, cos, sin_signed, quarter):
    lane = lax.broadcasted_iota(jnp.int32, t.shape, 1)
    width = t.shape[1]
    partner = jnp.where(lane % (2 * quarter) < quarter,
                        pltpu.roll(t, width - quarter, 1), pltpu.roll(t, quarter, 1))
    return t * cos + partner * sin_signed


def _ssd_conv_body(x_ref, w_ref, b_ref, o_ref, *, seq, transpose, chunk):
    w = w_ref[...]
    bias = b_ref[...]
    half = SSD_CONV // 2
    for s in range(0, seq, chunk):
        lo, hi = max(s - 8, 0), min(s + chunk + 8, seq)
        xa = x_ref[lo:hi, :]
        n = hi - lo
        t_glob = lax.broadcasted_iota(jnp.int32, (n, 1), 0) + lo
        acc = xa * w[half:half + 1, :]
        for kk in range(SSD_CONV):
            d = kk - half
            if d == 0:
                continue
            shifted = pltpu.roll(xa, (-d) % n, 0)
            valid = jnp.logical_and(t_glob + d >= 0, t_glob + d < seq)
            acc = acc + jnp.where(valid, shifted, 0.0) * w[kk:kk + 1, :]
        y = _silu(acc + bias)[s - lo:s - lo + chunk, :]
        if transpose:
            o_ref[:, s:s + chunk] = y.T.astype(o_ref.dtype)
        else:
            o_ref[s:s + chunk, :] = y.astype(o_ref.dtype)


def _ssd_conv(proj, conv_w, conv_b, *, bsz, seq, col0, ncols, conv_col0, transpose, out_dtype):
    chunk = min(512, seq)
    if transpose:
        out_shape = jax.ShapeDtypeStruct((bsz, ncols * LANE, seq), out_dtype)
        out_spec = pl.BlockSpec((None, LANE, seq), lambda b, j: (b, j, 0))
    else:
        out_shape = jax.ShapeDtypeStruct((bsz * seq, ncols * LANE), out_dtype)
        out_spec = pl.BlockSpec((seq, LANE), lambda b, j: (b, j))
    return pl.pallas_call(
        functools.partial(_ssd_conv_body, seq=seq, transpose=transpose, chunk=chunk),
        grid=(bsz, ncols),
        in_specs=[pl.BlockSpec((seq, LANE), lambda b, j: (b, col0 + j)),
                  pl.BlockSpec((SSD_CONV, LANE), lambda b, j: (0, conv_col0 + j)),
                  pl.BlockSpec((1, LANE), lambda b, j: (0, conv_col0 + j))],
        out_specs=out_spec, out_shape=out_shape,
        compiler_params=_cparams(("parallel", "parallel")), name="ssd_conv")(proj, conv_w, conv_b)


def _upper_ones(n):
    r = lax.broadcasted_iota(jnp.int32, (n, n), 0)
    c = lax.broadcasted_iota(jnp.int32, (n, n), 1)
    return (r <= c).astype(jnp.float32)


def _head_lane_select(vals, width, shape):
    lane = lax.broadcasted_iota(jnp.int32, shape, len(shape) - 1)
    out = jnp.broadcast_to(vals[-1], shape)
    for r in range(len(vals) - 2, -1, -1):
        out = jnp.where(lane < (r + 1) * width, vals[r], out)
    return out


def _ssd_row_decay(dt_row_ref, bias_row_ref, aneg_row_ref):
    dt_r = _softplus(dt_row_ref[...] + bias_row_ref[...])
    a_r = dt_r * aneg_row_ref[...]
    cs_r = _f32_dot(a_r, _upper_ones(a_r.shape[1]))
    return dt_r, a_r, cs_r


def _ssd_bwd_state_body(bt_ref, x_ref, dt_row_ref, bias_row_ref, aneg_row_ref, s0_ref,
                        s_after_ref, s_fin_ref, s_scr, *, nc):
    k = pl.program_id(2)
    nh = SSD_HEADS_PER_GROUP

    @pl.when(k == 0)
    def _():
        s_scr[...] = s0_ref[...]

    s_after_ref[...] = s_scr[...]
    dt_r, a_r, cs_r = _ssd_row_decay(dt_row_ref, bias_row_ref, aneg_row_ref)
    t = a_r.shape[1]
    ecs_r = cs_r - a_r
    bt = bt_ref[...].astype(jnp.float32)
    xb = x_ref[...].astype(jnp.bfloat16)
    upd = []
    tot = []
    for r in range(nh):
        w_row = jnp.exp(ecs_r[nh + r:nh + r + 1, :]) * dt_r[nh + r:nh + r + 1, :]
        upd.append(_bf16_dot(bt * w_row, xb))
        tot.append(jnp.exp(cs_r[nh + r:nh + r + 1, t - 1:t]))
    shape = s_scr.shape
    s_new = _head_lane_select(tot, SSD_HEAD_DIM, shape) * s_scr[...] + _head_lane_select(upd, SSD_HEAD_DIM, shape)
    s_scr[...] = s_new

    @pl.when(k == nc - 1)
    def _():
        s_fin_ref[...] = s_new


def _ssd_bwd_state(bt, xs, dt_row, bias_row, aneg_row, s0, *, bsz, seq):
    nc = seq // SCAN_T
    g = SSD_GROUPS
    rev = lambda k: nc - 1 - k
    return pl.pallas_call(
        functools.partial(_ssd_bwd_state_body, nc=nc),
        grid=(bsz, g, nc),
        in_specs=[pl.BlockSpec((None, SSD_STATE, SCAN_T), lambda b, gi, k: (b, gi, rev(k))),
                  pl.BlockSpec((SCAN_T, GROUP_W), lambda b, gi, k: (b * nc + rev(k), gi)),
                  pl.BlockSpec((None, None, N_DIR_HEADS, SCAN_T), lambda b, gi, k: (b, gi, 0, rev(k))),
                  pl.BlockSpec((None, N_DIR_HEADS, 1), lambda b, gi, k: (gi, 0, 0)),
                  pl.BlockSpec((None, N_DIR_HEADS, 1), lambda b, gi, k: (gi, 0, 0)),
                  pl.BlockSpec((None, None, SSD_STATE, GROUP_W), lambda b, gi, k: (b, gi, 0, 0))],
        out_specs=[pl.BlockSpec((None, None, None, SSD_STATE, GROUP_W), lambda b, gi, k: (b, gi, rev(k), 0, 0)),
                   pl.BlockSpec((None, None, SSD_STATE, GROUP_W), lambda b, gi, k: (b, gi, 0, 0))],
        out_shape=[jax.ShapeDtypeStruct((bsz, g, nc, SSD_STATE, GROUP_W), jnp.float32),
                   jax.ShapeDtypeStruct((bsz, g, SSD_STATE, GROUP_W), jnp.float32)],
        scratch_shapes=[pltpu.VMEM((SSD_STATE, GROUP_W), jnp.float32)],
        compiler_params=_cparams(("parallel", "parallel", "arbitrary")),
        name="ssd_bwd_state")(bt, xs, dt_row, bias_row, aneg_row, s0)


def _ssd_fwd_body(c_ref, bt_ref, x_ref, z_ref, dt_row_ref, bias_row_ref, aneg_row_ref, dskip_ref, nw_ref,
                  s0_ref, s_after_ref, o_ref, s_fin_ref, s_scr, *, nc):
    k = pl.program_id(2)
    nh = SSD_HEADS_PER_GROUP

    @pl.when(k == 0)
    def _():
        s_scr[...] = s0_ref[...]

    dt_r, a_r, cs_r = _ssd_row_decay(dt_row_ref, bias_row_ref, aneg_row_ref)
    t = a_r.shape[1]
    ecs_r = cs_r - a_r
    n_vec = 2 * N_DIR_HEADS
    cols = jnp.concatenate([cs_r, ecs_r, jnp.zeros((LANE - n_vec, t), jnp.float32)], axis=0).T
    cs_c = cols[:, :N_DIR_HEADS]
    ecs_c = cols[:, N_DIR_HEADS:n_vec]

    cq = c_ref[...]
    bt = bt_ref[...]
    x = x_ref[...]
    xb = x.astype(jnp.bfloat16)
    scores = jnp.dot(cq, bt, preferred_element_type=jnp.float32)
    row = lax.broadcasted_iota(jnp.int32, (t, t), 0)
    col = lax.broadcasted_iota(jnp.int32, (t, t), 1)
    neg_inf = jnp.float32(-jnp.inf)
    btf = bt.astype(jnp.float32)

    intra, e_f, e_b, upd, tot = [], [], [], [], []
    for r in range(nh):
        fr, br = r, nh + r
        seg_f = jnp.where(row >= col, cs_c[:, fr:fr + 1] - cs_r[fr:fr + 1, :], neg_inf)
        seg_b = jnp.where(col >= row, ecs_r[br:br + 1, :] - ecs_c[:, br:br + 1], neg_inf)
        wgt = jnp.exp(seg_f) * dt_r[fr:fr + 1, :] + jnp.exp(seg_b) * dt_r[br:br + 1, :]
        intra.append(_bf16_dot(scores * wgt, xb))
        e_f.append(jnp.exp(cs_c[:, fr:fr + 1]))
        e_b.append(jnp.exp(cs_r[br:br + 1, t - 1:t] - ecs_c[:, br:br + 1]))
        tot_f = cs_r[fr:fr + 1, t - 1:t]
        w_row = jnp.exp(tot_f - cs_r[fr:fr + 1, :]) * dt_r[fr:fr + 1, :]
        upd.append(_bf16_dot(btf * w_row, xb))
        tot.append(jnp.exp(tot_f))

    shape = x.shape
    s_f = s_scr[...]
    y = _head_lane_select(intra, SSD_HEAD_DIM, shape)
    y = y + _head_lane_select(e_f, SSD_HEAD_DIM, shape) * _bf16_dot(cq, s_f)
    y = y + _head_lane_select(e_b, SSD_HEAD_DIM, shape) * _bf16_dot(cq, s_after_ref[...])
    y = (y + dskip_ref[...] * x) * _silu(z_ref[...])
    y = y * lax.rsqrt(jnp.mean(y * y, axis=-1, keepdims=True) + NORM_EPS) * nw_ref[...]
    o_ref[...] = y.astype(o_ref.dtype)

    s_new = (_head_lane_select(tot, SSD_HEAD_DIM, s_f.shape) * s_f
             + _head_lane_select(upd, SSD_HEAD_DIM, s_f.shape))
    s_scr[...] = s_new

    @pl.when(k == nc - 1)
    def _():
        s_fin_ref[...] = s_new


def _ssd_fwd(cm, bt, xs, proj, dt_row, bias_row, aneg_row, dskip, norm_w, s0, s_after, *, bsz, seq):
    nc = seq // SCAN_T
    g = SSD_GROUPS
    zcol = COL_SSD_Z * LANE // GROUP_W
    vec_row = pl.BlockSpec((None, N_DIR_HEADS, 1), lambda b, gi, k: (gi, 0, 0))
    return pl.pallas_call(
        functools.partial(_ssd_fwd_body, nc=nc),
        grid=(bsz, g, nc),
        in_specs=[pl.BlockSpec((SCAN_T, SSD_STATE), lambda b, gi, k: (b * nc + k, gi)),
                  pl.BlockSpec((None, SSD_STATE, SCAN_T), lambda b, gi, k: (b, gi, k)),
                  pl.BlockSpec((SCAN_T, GROUP_W), lambda b, gi, k: (b * nc + k, gi)),
                  pl.BlockSpec((SCAN_T, GROUP_W), lambda b, gi, k: (b * nc + k, zcol + gi)),
                  pl.BlockSpec((None, None, N_DIR_HEADS, SCAN_T), lambda b, gi, k: (b, gi, 0, k)),
                  vec_row, vec_row,
                  pl.BlockSpec((1, GROUP_W), lambda b, gi, k: (0, gi)),
                  pl.BlockSpec((1, GROUP_W), lambda b, gi, k: (0, gi)),
                  pl.BlockSpec((None, None, SSD_STATE, GROUP_W), lambda b, gi, k: (b, gi, 0, 0)),
                  pl.BlockSpec((None, None, None, SSD_STATE, GROUP_W), lambda b, gi, k: (b, gi, k, 0, 0))],
        out_specs=[pl.BlockSpec((SCAN_T, GROUP_W), lambda b, gi, k: (b * nc + k, gi)),
                   pl.BlockSpec((None, None, SSD_STATE, GROUP_W), lambda b, gi, k: (b, gi, 0, 0))],
        out_shape=[jax.ShapeDtypeStruct((bsz * seq, SSD_WIDTH), jnp.bfloat16),
                   jax.ShapeDtypeStruct((bsz, g, SSD_STATE, GROUP_W), jnp.float32)],
        scratch_shapes=[pltpu.VMEM((SSD_STATE, GROUP_W), jnp.float32)],
        compiler_params=_cparams(("parallel", "parallel", "arbitrary")), name="ssd_fwd")(
            cm, bt, xs, proj, dt_row, bias_row, aneg_row, dskip, norm_w, s0, s_after)


def _ssd_group(proj, proj_dt, proj_c, proj_c_dt, lw, *, bsz, seq, n_ctx, need_ctx_out):
    g, nh = SSD_GROUPS, SSD_HEADS_PER_GROUP
    conv_w = lw['ssd_conv_w']
    conv_b = lw['ssd_conv_b'][None, :]

    def per_group(v):
        return v.astype(jnp.float32).reshape(2, g, nh).transpose(1, 0, 2).reshape(g, 2 * nh)

    bias = per_group(lw['ssd_dt_bias'])
    aneg = per_group(-jnp.exp(lw['ssd_a_log'].astype(jnp.float32)))
    bias_row, aneg_row = bias[:, :, None], aneg[:, :, None]
    dskip = jnp.repeat(lw['ssd_d'].astype(jnp.float32), SSD_HEAD_DIM)[None, :]
    norm_w = lw['ssd_norm_w'][None, :]

    def prepare(p, p_dt, n):
        conv = functools.partial(_ssd_conv, p, conv_w, conv_b, bsz=bsz, seq=n)
        xs = conv(col0=COL_SSD_X, ncols=SSD_WIDTH // LANE, conv_col0=0, transpose=False, out_dtype=jnp.float32)
        bt = conv(col0=COL_SSD_B, ncols=g, conv_col0=SSD_WIDTH // LANE, transpose=True, out_dtype=jnp.bfloat16)
        cm = conv(col0=COL_SSD_C, ncols=g, conv_col0=SSD_WIDTH // LANE + g, transpose=False,
                  out_dtype=jnp.bfloat16)
        dt_row = p_dt[:, :DT_WIDTH].reshape(bsz, n, 2, g, nh).transpose(0, 3, 2, 4, 1).reshape(bsz, g, 2 * nh, n)
        return xs, bt, cm, dt_row

    zeros = jnp.zeros((bsz, g, SSD_STATE, GROUP_W), jnp.float32)
    xs_c, bt_c, cm_c, dtr_c = prepare(proj_c, proj_c_dt, n_ctx)
    xs, bt, cm, dt_row = prepare(proj, proj_dt, seq)
    sa_c, sb0 = _ssd_bwd_state(bt_c, xs_c, dtr_c, bias_row, aneg_row, zeros, bsz=bsz, seq=n_ctx)
    fwd = functools.partial(_ssd_fwd, bias_row=bias_row, aneg_row=aneg_row, dskip=dskip, norm_w=norm_w, bsz=bsz)
    y_c, sf0 = fwd(cm_c, bt_c, xs_c, proj_c, dtr_c, s0=zeros, s_after=sa_c, seq=n_ctx)
    sa, _ = _ssd_bwd_state(bt, xs, dt_row, bias_row, aneg_row, sb0, bsz=bsz, seq=seq)
    y, _ = fwd(cm, bt, xs, proj, dt_row, s0=sf0, s_after=sa, seq=seq)
    return y, (y_c if need_ctx_out else None)


def _ret_prep_body(q_ref, k_ref, cos_ref, sin_ref, qo_ref, kto_ref, *, use_rope):
    q = q_ref[...]
    kk = k_ref[...]
    if use_rope:
        q = _rope(q, cos_ref[...], sin_ref[...], RET_HEAD_DIM // 4)
        kk = _rope(kk, cos_ref[...], sin_ref[...], RET_HEAD_DIM // 4)
    kk = kk * (RET_HEAD_DIM ** -0.5)
    qo_ref[...] = q.astype(qo_ref.dtype)
    kto_ref[...] = kk.T.astype(kto_ref.dtype)


def _ret_prep(proj, cos, sin, *, bsz, seq, use_rope):
    tr = min(512, seq)
    nr = seq // tr
    return pl.pallas_call(
        functools.partial(_ret_prep_body, use_rope=use_rope),
        grid=(bsz, RET_HEADS, nr),
        in_specs=[pl.BlockSpec((tr, LANE), lambda b, h, i: (b * nr + i, COL_RET_Q + h)),
                  pl.BlockSpec((tr, LANE), lambda b, h, i: (b * nr + i, COL_RET_K + h)),
                  pl.BlockSpec((tr, LANE), lambda b, h, i: (i, 0)),
                  pl.BlockSpec((tr, LANE), lambda b, h, i: (i, 0))],
        out_specs=[pl.BlockSpec((tr, LANE), lambda b, h, i: (b * nr + i, h)),
                   pl.BlockSpec((None, LANE, tr), lambda b, h, i: (b, h, i))],
        out_shape=[jax.ShapeDtypeStruct((bsz * seq, RET_WIDTH), jnp.bfloat16),
                   jax.ShapeDtypeStruct((bsz, RET_WIDTH, seq), jnp.bfloat16)],
        compiler_params=_cparams(("parallel", "parallel", "parallel")), name="ret_prep")(proj, proj, cos, sin)


def _ret_bwd_state_body(lg_ref, kt_ref, v_ref, s0_ref, s_after_ref, s_fin_ref, s_scr, *, nc):
    h = pl.program_id(1)
    k = pl.program_id(2)

    @pl.when(k == 0)
    def _():
        s_scr[...] = s0_ref[...]

    s_after_ref[...] = s_scr[...]
    lgb = jnp.full((1, 1), lg_ref[1, h], jnp.float32)
    t = kt_ref.shape[1]
    j = lax.broadcasted_iota(jnp.int32, (1, t), 1).astype(jnp.float32)
    kt = kt_ref[...].astype(jnp.float32) * jnp.exp(j * lgb)
    s_new = jnp.exp(t * lgb) * s_scr[...] + _bf16_dot(kt, v_ref[...])
    s_scr[...] = s_new

    @pl.when(k == nc - 1)
    def _():
        s_fin_ref[...] = s_new


def _ret_bwd_state(lg, kt, proj, s0, *, bsz, seq):
    nc = seq // SCAN_T
    rev = lambda k: nc - 1 - k
    grid_spec = pltpu.PrefetchScalarGridSpec(
        num_scalar_prefetch=0, grid=(bsz, RET_HEADS, nc),
        in_specs=[pl.BlockSpec(memory_space=pltpu.SMEM),
                  pl.BlockSpec((None, LANE, SCAN_T), lambda b, h, k: (b, h, rev(k))),
                  pl.BlockSpec((SCAN_T, LANE), lambda b, h, k: (b * nc + rev(k), COL_RET_V + h)),
                  pl.BlockSpec((None, None, LANE, LANE), lambda b, h, k: (b, h, 0, 0))],
        out_specs=[pl.BlockSpec((None, None, None, LANE, LANE), lambda b, h, k: (b, h, rev(k), 0, 0)),
                   pl.BlockSpec((None, None, LANE, LANE), lambda b, h, k: (b, h, 0, 0))],
        scratch_shapes=[pltpu.VMEM((LANE, LANE), jnp.float32)])
    return pl.pallas_call(
        functools.partial(_ret_bwd_state_body, nc=nc), grid_spec=grid_spec,
        out_shape=[jax.ShapeDtypeStruct((bsz, RET_HEADS, nc, LANE, LANE), jnp.float32),
                   jax.ShapeDtypeStruct((bsz, RET_HEADS, LANE, LANE), jnp.float32)],
        compiler_params=_cparams(("parallel", "parallel", "arbitrary")), name="ret_bwd_state")(lg, kt, proj, s0)


def _ret_fwd_body(lg_ref, q_ref, kt_ref, v_ref, g_ref, nw_ref, s0_ref, s_after_ref,
                  o_ref, s_fin_ref, s_scr, d_scr, *, nc):
    h = pl.program_id(1)
    k = pl.program_id(2)
    lgf = jnp.full((1, 1), lg_ref[0, h], jnp.float32)
    lgb = jnp.full((1, 1), lg_ref[1, h], jnp.float32)
    t = q_ref.shape[0]

    @pl.when(k == 0)
    def _():
        s_scr[...] = s0_ref[...]
        row = lax.broadcasted_iota(jnp.int32, (t, t), 0)
        col = lax.broadcasted_iota(jnp.int32, (t, t), 1)
        dist = (row - col).astype(jnp.float32)
        neg_inf = jnp.float32(-jnp.inf)
        d_scr[...] = (jnp.exp(jnp.where(row >= col, dist * lgf, neg_inf))
                      + jnp.exp(jnp.where(col >= row, -dist * lgb, neg_inf)))

    q = q_ref[...]
    kt = kt_ref[...]
    v = v_ref[...].astype(jnp.bfloat16)
    scores = jnp.dot(q, kt, preferred_element_type=jnp.float32)
    i = lax.broadcasted_iota(jnp.int32, (t, 1), 0).astype(jnp.float32)
    j = lax.broadcasted_iota(jnp.int32, (1, t), 1).astype(jnp.float32)
    s_f = s_scr[...]
    y = _bf16_dot(scores * d_scr[...], v)
    y = y + jnp.exp((i + 1.0) * lgf) * _bf16_dot(q, s_f)
    y = y + jnp.exp((t - i) * lgb) * _bf16_dot(q, s_after_ref[...])
    y = y * lax.rsqrt(jnp.mean(y * y, axis=-1, keepdims=True) + NORM_EPS) * nw_ref[...]
    o_ref[...] = (_silu(g_ref[...]) * y).astype(o_ref.dtype)

    ktw = kt.astype(jnp.float32) * jnp.exp((t - 1.0 - j) * lgf)
    s_new = jnp.exp(t * lgf) * s_f + _bf16_dot(ktw, v)
    s_scr[...] = s_new

    @pl.when(k == nc - 1)
    def _():
        s_fin_ref[...] = s_new


def _ret_fwd(lg, q, kt, proj, norm_w, s0, s_after, *, bsz, seq):
    nc = seq // SCAN_T
    grid_spec = pltpu.PrefetchScalarGridSpec(
        num_scalar_prefetch=0, grid=(bsz, RET_HEADS, nc),
        in_specs=[pl.BlockSpec(memory_space=pltpu.SMEM),
                  pl.BlockSpec((SCAN_T, LANE), lambda b, h, k: (b * nc + k, h)),
                  pl.BlockSpec((None, LANE, SCAN_T), lambda b, h, k: (b, h, k)),
                  pl.BlockSpec((SCAN_T, LANE), lambda b, h, k: (b * nc + k, COL_RET_V + h)),
                  pl.BlockSpec((SCAN_T, LANE), lambda b, h, k: (b * nc + k, COL_RET_G + h)),
                  pl.BlockSpec((1, LANE), lambda b, h, k: (0, h)),
                  pl.BlockSpec((None, None, LANE, LANE), lambda b, h, k: (b, h, 0, 0)),
                  pl.BlockSpec((None, None, None, LANE, LANE), lambda b, h, k: (b, h, k, 0, 0))],
        out_specs=[pl.BlockSpec((SCAN_T, LANE), lambda b, h, k: (b * nc + k, h)),
                   pl.BlockSpec((None, None, LANE, LANE), lambda b, h, k: (b, h, 0, 0))],
        scratch_shapes=[pltpu.VMEM((LANE, LANE), jnp.float32), pltpu.VMEM((SCAN_T, SCAN_T), jnp.float32)])
    return pl.pallas_call(
        functools.partial(_ret_fwd_body, nc=nc), grid_spec=grid_spec,
        out_shape=[jax.ShapeDtypeStruct((bsz * seq, RET_WIDTH), jnp.bfloat16),
                   jax.ShapeDtypeStruct((bsz, RET_HEADS, LANE, LANE), jnp.float32)],
        compiler_params=_cparams(("parallel", "parallel", "arbitrary")), name="ret_fwd")(
            lg, q, kt, proj, proj, norm_w, s0, s_after)


def _ret_group(proj, proj_c, lw, rope, *, bsz, seq, n_ctx, need_ctx_out):
    cos, sin = rope
    lg = jax.nn.log_sigmoid(lw['ret_decay_logit'].astype(jnp.float32))
    norm_w = lw['ret_norm_w'][None, :]
    zeros = jnp.zeros((bsz, RET_HEADS, LANE, LANE), jnp.float32)
    q_c, kt_c = _ret_prep(proj_c, cos, sin, bsz=bsz, seq=n_ctx, use_rope=False)
    q, kt = _ret_prep(proj, cos, sin, bsz=bsz, seq=seq, use_rope=True)
    sa_c, sb0 = _ret_bwd_state(lg, kt_c, proj_c, zeros, bsz=bsz, seq=n_ctx)
    y_c, sf0 = _ret_fwd(lg, q_c, kt_c, proj_c, norm_w, zeros, sa_c, bsz=bsz, seq=n_ctx)
    sa, _ = _ret_bwd_state(lg, kt, proj, sb0, bsz=bsz, seq=seq)
    y, _ = _ret_fwd(lg, q, kt, proj, norm_w, sf0, sa, bsz=bsz, seq=seq)
    return y, (y_c if need_ctx_out else None)


def _diff_prep_body(q_ref, k_ref, v_ref, cos_ref, sin_ref, qt1_ref, qt2_ref, ko_ref, vto_ref, *, use_rope):
    q = q_ref[...]
    kk = k_ref[...]
    if use_rope:
        q = _rope(q, cos_ref[...], sin_ref[...], DIFF_QK_DIM // 4)
        kk = _rope(kk, cos_ref[...], sin_ref[...], DIFF_QK_DIM // 4)
    qt = (q * (DIFF_QK_DIM ** -0.5 * LOG2_E)).T
    sub = lax.broadcasted_iota(jnp.int32, qt.shape, 0)
    qt1_ref[...] = jnp.where(sub < DIFF_QK_DIM, qt, 0.0).astype(qt1_ref.dtype)
    qt2_ref[...] = jnp.where(sub >= DIFF_QK_DIM, qt, 0.0).astype(qt2_ref.dtype)
    ko_ref[...] = kk.astype(ko_ref.dtype)
    vto_ref[...] = v_ref[...].T.astype(vto_ref.dtype)


def _diff_prep(proj, cos, sin, *, bsz, seq, use_rope):
    tr = KV_CHUNK
    nr = seq // tr
    t_spec = pl.BlockSpec((None, LANE, tr), lambda b, h, i: (b, h, i))
    t_shape = jax.ShapeDtypeStruct((bsz, DIFF_WIDTH, seq), jnp.bfloat16)
    return pl.pallas_call(
        functools.partial(_diff_prep_body, use_rope=use_rope),
        grid=(bsz, DIFF_HEADS, nr),
        in_specs=[pl.BlockSpec((tr, LANE), lambda b, h, i: (b * nr + i, COL_DIFF_Q + h)),
                  pl.BlockSpec((tr, LANE), lambda b, h, i: (b * nr + i, COL_DIFF_K + h)),
                  pl.BlockSpec((tr, LANE), lambda b, h, i: (b * nr + i, COL_DIFF_V + h)),
                  pl.BlockSpec((tr, LANE), lambda b, h, i: (i, 0)),
                  pl.BlockSpec((tr, LANE), lambda b, h, i: (i, 0))],
        out_specs=[t_spec, t_spec,
                   pl.BlockSpec((None, tr, LANE), lambda b, h, i: (b, i, h)),
                   pl.BlockSpec((None, None, LANE, tr), lambda b, h, i: (b, i, h, 0))],
        out_shape=[t_shape, t_shape,
                   jax.ShapeDtypeStruct((bsz, seq, DIFF_WIDTH), jnp.bfloat16),
                   jax.ShapeDtypeStruct((bsz, nr, DIFF_WIDTH, tr), jnp.bfloat16)],
        compiler_params=_cparams(("parallel", "parallel", "parallel")), name="diff_prep")(
            proj, proj, proj, cos, sin)


def _diff_attn_body(lam_ref, qt1_ref, qt2_ref, k_ref, vt_ref, nw_ref, o_ref, acc1, acc2, s_scr, *,
                    chunks, out_scale):
    tk = chunks * KV_CHUNK
    n_steps = k_ref.shape[0] // tk
    tq = qt1_ref.shape[1]
    qts = (qt1_ref[...], qt2_ref[...])
    accs = (acc1, acc2)
    for acc in accs:
        acc[...] = jnp.zeros_like(acc)

    def scores(step, slot):
        start = pl.multiple_of(step * tk, tk)
        kk = k_ref[pl.ds(start, tk), :]
        for mp in range(2):
            s_scr[slot, mp] = jnp.dot(kk, qts[mp], preferred_element_type=jnp.float32)

    def consume(step, slot, carry):
        new = []
        for mp in range(2):
            m, l = carry[mp]
            s = s_scr[slot, mp]
            m_new = jnp.maximum(m, jnp.max(s, axis=0, keepdims=True))
            alpha = jnp.exp2(m - m_new)
            p = jnp.exp2(s - m_new)
            l_new = alpha * l + jnp.sum(p, axis=0, keepdims=True)
            pb = p.astype(jnp.bfloat16)
            pv = jnp.dot(vt_ref[step * chunks], pb[:KV_CHUNK], preferred_element_type=jnp.float32)
            for c in range(1, chunks):
                pv = pv + jnp.dot(vt_ref[step * chunks + c], pb[c * KV_CHUNK:(c + 1) * KV_CHUNK],
                                  preferred_element_type=jnp.float32)
            accs[mp][...] = alpha * accs[mp][...] + pv
            new.append((m_new, l_new))
        return tuple(new)

    def pair(jj, carry):
        j = 2 * jj
        scores(j + 1, 1)
        carry = consume(j, 0, carry)
        scores(jnp.minimum(j + 2, n_steps - 1), 0)
        return consume(j + 1, 1, carry)

    init = tuple((jnp.full((1, tq), -jnp.inf, jnp.float32), jnp.zeros((1, tq), jnp.float32)) for _ in range(2))
    scores(0, 0)
    carry = lax.fori_loop(0, n_steps // 2, pair, init)
    if n_steps % 2:
        carry = consume(n_steps - 1, 0, carry)
    (m1, l1), (m2, l2) = carry
    o = acc1[...] / l1 - lam_ref[0] * (acc2[...] / l2)
    o = o * lax.rsqrt(jnp.mean(o * o, axis=0, keepdims=True) + NORM_EPS)
    o_ref[...] = (o.T * nw_ref[...] * out_scale).astype(o_ref.dtype)


def _diff_attn(lam, qt1, qt2, k_all, vt_all, norm_w, *, bsz, n_q, lambda_init, tq):
    nq = n_q // tq
    kv_len = k_all.shape[1]
    n_chunks = kv_len // KV_CHUNK
    chunks = 3 if n_chunks % 3 == 0 else 1
    grid_spec = pltpu.PrefetchScalarGridSpec(
        num_scalar_prefetch=0, grid=(bsz, DIFF_HEADS, nq),
        in_specs=[pl.BlockSpec(memory_space=pltpu.SMEM),
                  pl.BlockSpec((None, LANE, tq), lambda b, h, i: (b, h, i)),
                  pl.BlockSpec((None, LANE, tq), lambda b, h, i: (b, h, i)),
                  pl.BlockSpec((None, kv_len, LANE), lambda b, h, i: (b, 0, h)),
                  pl.BlockSpec((None, n_chunks, LANE, KV_CHUNK), lambda b, h, i: (b, 0, h, 0)),
                  pl.BlockSpec((1, LANE), lambda b, h, i: (0, 0))],
        out_specs=pl.BlockSpec((tq, LANE), lambda b, h, i: (b * nq + i, h)),
        scratch_shapes=[pltpu.VMEM((LANE, tq), jnp.float32), pltpu.VMEM((LANE, tq), jnp.float32),
                        pltpu.VMEM((2, 2, chunks * KV_CHUNK, tq), jnp.float32)])
    return pl.pallas_call(
        functools.partial(_diff_attn_body, chunks=chunks, out_scale=1.0 - lambda_init), grid_spec=grid_spec,
        out_shape=jax.ShapeDtypeStruct((bsz * n_q, DIFF_WIDTH), jnp.bfloat16),
        compiler_params=_cparams(("parallel", "parallel", "arbitrary")), name="diff_attn")(
            lam, qt1, qt2, k_all, vt_all, norm_w)


def _diff_group(proj, proj_c, lw, rope, *, bsz, seq, n_ctx, lambda_init, need_ctx_out):
    cos, sin = rope
    lv = lw['diff_lambda'].astype(jnp.float32)
    lam = (jnp.exp(jnp.sum(lv[0] * lv[1])) - jnp.exp(jnp.sum(lv[2] * lv[3])) + lambda_init).reshape(1)
    norm_w = lw['diff_norm_w'][None, :]
    qt1, qt2, k_l, vt_l = _diff_prep(proj, cos, sin, bsz=bsz, seq=seq, use_rope=True)
    qt1c, qt2c, k_c, vt_c = _diff_prep(proj_c, cos, sin, bsz=bsz, seq=n_ctx, use_rope=False)
    k_all = jnp.concatenate([k_l, k_c], axis=1)
    vt_all = jnp.concatenate([vt_l, vt_c], axis=1)
    y = _diff_attn(lam, qt1, qt2, k_all, vt_all, norm_w, bsz=bsz, n_q=seq, lambda_init=lambda_init,
                   tq=min(1024, seq))
    y_c = None
    if need_ctx_out:
        y_c = _diff_attn(lam, qt1c, qt2c, k_c, vt_c, norm_w, bsz=bsz, n_q=n_ctx, lambda_init=lambda_init,
                         tq=n_ctx)
    return y, y_c


def _mixers(proj, proj_dt, proj_c, proj_c_dt, lw, *, bsz, seq, n_ctx, lambda_init, need_ctx_out):
    rope_ret = _rope_tables(seq, RET_HEAD_DIM, 1)
    rope_diff = _rope_tables(seq, DIFF_QK_DIM, 2)
    r_l, r_c = _ret_group(proj, proj_c, lw, rope_ret, bsz=bsz, seq=seq, n_ctx=n_ctx, need_ctx_out=need_ctx_out)
    s_l, s_c = _ssd_group(proj, proj_dt, proj_c, proj_c_dt, lw, bsz=bsz, seq=seq, n_ctx=n_ctx,
                          need_ctx_out=need_ctx_out)
    d_l, d_c = _diff_group(proj, proj_c, lw, rope_diff, bsz=bsz, seq=seq, n_ctx=n_ctx,
                           lambda_init=lambda_init, need_ctx_out=need_ctx_out)
    mix = jnp.concatenate([r_l, s_l, d_l], axis=-1)
    mix_c = jnp.concatenate([r_c, s_c, d_c], axis=-1) if need_ctx_out else None
    return mix, mix_c


def _gather_rows_body(idx_ref, src_hbm, o_ref, buf, sem, *, tm, n_steps):
    i = pl.program_id(0)

    def row_copy(step, slot, r):
        return pltpu.make_async_copy(src_hbm.at[pl.ds(idx_ref[step * tm + r], 1)],
                                     buf.at[slot, pl.ds(r, 1)], sem.at[slot])

    def start_block(step, slot):
        def body(r, carry):
            row_copy(step, slot, r).start()
            return carry
        lax.fori_loop(0, tm, body, 0)

    def wait_block(step, slot):
        def body(r, carry):
            row_copy(step, slot, r).wait()
            return carry
        lax.fori_loop(0, tm, body, 0)

    slot = i % 2

    @pl.when(i == 0)
    def _():
        start_block(0, 0)

    @pl.when(i + 1 < n_steps)
    def _():
        start_block(i + 1, 1 - slot)

    wait_block(i, slot)
    o_ref[...] = buf[slot].astype(o_ref.dtype)


def _gather_rows(idx, src, *, tm, out_dtype):
    n_rows = idx.shape[0]
    d = src.shape[1]
    n_steps = n_rows // tm
    grid_spec = pltpu.PrefetchScalarGridSpec(
        num_scalar_prefetch=1, grid=(n_steps,),
        in_specs=[pl.BlockSpec(memory_space=pl.ANY)],
        out_specs=pl.BlockSpec((tm, d), lambda i, idx_ref: (i, 0)),
        scratch_shapes=[pltpu.VMEM((2, tm, d), src.dtype), pltpu.SemaphoreType.DMA((2,))])
    return pl.pallas_call(
        functools.partial(_gather_rows_body, tm=tm, n_steps=n_steps), grid_spec=grid_spec,
        out_shape=jax.ShapeDtypeStruct((n_rows, d), out_dtype),
        compiler_params=_cparams(("arbitrary",)), name="moe_gather_rows")(idx, src)


def _moe_combine_body(pos_ref, y_hbm, g_ref, res_ref, gate_ref, o_ref, buf, sem, *, tm, n_steps):
    i = pl.program_id(0)

    def row_copy(step, slot, r, kk):
        src_row = pos_ref[(step * tm + r) * TOP_K + kk]
        return pltpu.make_async_copy(y_hbm.at[pl.ds(src_row, 1)], buf.at[slot, kk, pl.ds(r, 1)], sem.at[slot])

    def start_block(step, slot):
        def body(r, carry):
            for kk in range(TOP_K):
                row_copy(step, slot, r, kk).start()
            return carry
        lax.fori_loop(0, tm, body, 0)

    def wait_block(step, slot):
        def body(r, carry):
            for kk in range(TOP_K):
                row_copy(step, slot, r, kk).wait()
            return carry
        lax.fori_loop(0, tm, body, 0)

    slot = i % 2

    @pl.when(i == 0)
    def _():
        start_block(0, 0)

    @pl.when(i + 1 < n_steps)
    def _():
        start_block(i + 1, 1 - slot)

    wait_block(i, slot)
    g = g_ref[...]
    f = g[:, 0:1] * buf[slot, 0]
    for kk in range(1, TOP_K):
        f = f + g[:, kk:kk + 1] * buf[slot, kk]
    o_ref[...] = res_ref[...] + gate_ref[...] * f


def _moe_combine(pos, y_rows, gates, res, gate_vec, rows_per_gate, *, tm=128):
    n_tok, d = res.shape
    n_steps = n_tok // tm
    grid_spec = pltpu.PrefetchScalarGridSpec(
        num_scalar_prefetch=1, grid=(n_steps,),
        in_specs=[pl.BlockSpec(memory_space=pl.ANY),
                  pl.BlockSpec((tm, TOP_K), lambda i, p: (i, 0)),
                  pl.BlockSpec((tm, d), lambda i, p: (i, 0)),
                  pl.BlockSpec((None, 1, d), lambda i, p: ((i * tm) // rows_per_gate, 0, 0))],
        out_specs=pl.BlockSpec((tm, d), lambda i, p: (i, 0)),
        scratch_shapes=[pltpu.VMEM((2, TOP_K, tm, d), jnp.float32), pltpu.SemaphoreType.DMA((2,))])
    return pl.pallas_call(
        functools.partial(_moe_combine_body, tm=tm, n_steps=n_steps), grid_spec=grid_spec,
        out_shape=jax.ShapeDtypeStruct((n_tok, d), jnp.float32),
        compiler_params=_cparams(("arbitrary",)), name="moe_combine")(pos, y_rows, gates, res, gate_vec)


def _moe(h, router_w, w_gate, w_up, w_down, res, gate_vec, rows_per_gate):
    n_tok, d = h.shape
    n_exp = N_EXPERTS
    n_assign = n_tok * TOP_K
    logits = _matmul(h, [router_w], tm=512, tn=LANE, tk=d, out_dtype=jnp.float32, name="moe_router")[:, :n_exp]
    top_logits, top_idx = lax.top_k(logits, TOP_K)
    gates = jax.nn.softmax(top_logits, axis=-1)
    expert = top_idx.reshape(n_assign)
    order = jnp.argsort(expert)
    expert_s = expert[order]
    token_s = order // TOP_K
    counts = jnp.bincount(expert, length=n_exp)
    starts = jnp.cumsum(counts) - counts
    padded = (counts + MOE_BLOCK - 1) // MOE_BLOCK * MOE_BLOCK
    pad_ends = jnp.cumsum(padded)
    pad_starts = pad_ends - padded
    dest = pad_starts[expert_s] + jnp.arange(n_assign) - starts[expert_s]
    n_blocks = -(-n_assign // MOE_BLOCK) + n_exp
    src_tok = jnp.zeros((n_blocks * MOE_BLOCK,), jnp.int32).at[dest].set(token_s.astype(jnp.int32))
    rows = _gather_rows(src_tok, h, tm=MOE_BLOCK // 2, out_dtype=jnp.bfloat16)
    block_expert = jnp.minimum(
        jnp.searchsorted(pad_ends, jnp.arange(n_blocks) * MOE_BLOCK, side='right'), n_exp - 1
    ).astype(jnp.int32)
    t = _matmul(rows, [w_gate, w_up], tm=MOE_BLOCK, tn=512, tk=d, out_dtype=jnp.bfloat16,
                block_expert=block_expert, n_major=True, name="moe_gate_up")
    y_rows = _matmul(t, [w_down], tm=MOE_BLOCK, tn=1024, tk=w_down.shape[1], out_dtype=jnp.float32,
                     block_expert=block_expert, n_major=True, name="moe_down")
    pos = jnp.zeros((n_assign,), jnp.int32).at[order].set(dest.astype(jnp.int32))
    return _moe_combine(pos, y_rows, gates, res, gate_vec, rows_per_gate)


def kernel(x, c, ctx, c_ctx, ada_w, ada_b, norm_mix_w, norm_ffn_w, w_in, w_out, ret_decay_logit, ret_norm_w, ssd_conv_w, ssd_conv_b, ssd_dt_bias, ssd_a_log, ssd_d, ssd_norm_w, diff_lambda, diff_norm_w, dense_w_gate, dense_w_up, dense_w_down, moe_router, moe_w_gate, moe_w_up, moe_w_down, final_norm_w):
    bsz, seq, d = x.shape
    n_ctx = ctx.shape[1]
    assert n_ctx % SCAN_T == 0 and n_ctx % KV_CHUNK == 0 and seq % 1024 == 0
    bf16 = jnp.bfloat16

    xl = x.reshape(bsz * seq, d)
    xc = ctx.reshape(bsz * n_ctx, d)
    cond = jnp.concatenate([c, c_ctx[None, :], jnp.zeros((8 - bsz - 1, d), jnp.float32)], axis=0)

    for layer in range(DEPTH):
        last = layer == DEPTH - 1
        lambda_init = 0.8 - 0.6 * math.exp(-0.3 * layer)
        lw = {'ret_decay_logit': ret_decay_logit[layer], 'ret_norm_w': ret_norm_w[layer],
              'ssd_conv_w': ssd_conv_w[layer], 'ssd_conv_b': ssd_conv_b[layer],
              'ssd_dt_bias': ssd_dt_bias[layer], 'ssd_a_log': ssd_a_log[layer], 'ssd_d': ssd_d[layer],
              'ssd_norm_w': ssd_norm_w[layer], 'diff_lambda': diff_lambda[layer],
              'diff_norm_w': diff_norm_w[layer]}

        mod = _ada_modulation(cond, ada_w[layer], ada_b[layer][None, :])
        mod_l = mod[:bsz].reshape(bsz, 1, 6, d)
        mod_c = mod[bsz:bsz + 1].reshape(1, 1, 6, d)
        shift_m, scale_m, gate_m, shift_f, scale_f, gate_f = [mod_l[:, :, i] for i in range(6)]
        shift_mc, scale_mc, gate_mc, shift_fc, scale_fc, gate_fc = [mod_c[:, :, i] for i in range(6)]

        w_main, w_dt = _w_in_repack(w_in, layer)
        w_out_l = w_out[layer].astype(bf16)
        nmw = norm_mix_w[layer][None, :]
        nfw = norm_ffn_w[layer][None, :]

        h = _rmsnorm(xl, nmw, shift=shift_m, scale=scale_m, rows_per_vec=seq, out_dtype=bf16)
        hc = _rmsnorm(xc, nmw, shift=shift_mc, scale=scale_mc, rows_per_vec=bsz * n_ctx, out_dtype=bf16)
        proj = _matmul(h, [w_main], tm=1024, tn=1024, tk=d, out_dtype=jnp.float32, name="in_proj")
        proj_dt = _matmul(h, [w_dt], tm=1024, tn=DT_PAD, tk=d, out_dtype=jnp.float32, name="in_proj_dt")
        proj_c = _matmul(hc, [w_main], tm=512, tn=1024, tk=d, out_dtype=jnp.float32, name="in_proj_ctx")
        proj_c_dt = _matmul(hc, [w_dt], tm=512, tn=DT_PAD, tk=d, out_dtype=jnp.float32, name="in_proj_dt_ctx")
        mix, mix_c = _mixers(proj, proj_dt, proj_c, proj_c_dt, lw, bsz=bsz, seq=seq, n_ctx=n_ctx,
                             lambda_init=lambda_init, need_ctx_out=not last)
        xl = _matmul(mix, [w_out_l], tm=512, tn=1024, tk=MIX_WIDTH, out_dtype=jnp.float32,
                     res=xl, gate=gate_m, rows_per_gate=seq, name="out_proj")
        if not last:
            xc = _matmul(mix_c, [w_out_l], tm=512, tn=1024, tk=MIX_WIDTH // 2, out_dtype=jnp.float32,
                         res=xc, gate=gate_mc, rows_per_gate=bsz * n_ctx, name="out_proj_ctx")

        h = _rmsnorm(xl, nfw, shift=shift_f, scale=scale_f, rows_per_vec=seq,
                     out_dtype=bf16 if layer % 2 == 0 else jnp.float32)
        if not last:
            hc = _rmsnorm(xc, nfw, shift=shift_fc, scale=scale_fc, rows_per_vec=bsz * n_ctx, out_dtype=bf16)
        i = layer // 2
        if layer % 2 == 0:
            wg, wu, wd = dense_w_gate[i].astype(bf16), dense_w_up[i].astype(bf16), dense_w_down[i].astype(bf16)
            t = _matmul(h, [wg, wu], tm=2048, tn=256, tk=d, out_dtype=bf16, name="ffn_gate_up")
            xl = _matmul(t, [wd], tm=512, tn=512, tk=D_FF, out_dtype=jnp.float32, n_major=True,
                         res=xl, gate=gate_f, rows_per_gate=seq, name="ffn_down")
            if not last:
                tc = _matmul(hc, [wg, wu], tm=512, tn=256, tk=d, out_dtype=bf16, name="ffn_gate_up_ctx")
                xc = _matmul(tc, [wd], tm=512, tn=512, tk=D_FF, out_dtype=jnp.float32, n_major=True,
                             res=xc, gate=gate_fc, rows_per_gate=bsz * n_ctx, name="ffn_down_ctx")
        else:
            assert last, "expert layers that still carry a context stream are not supported"
            router = jnp.pad(moe_router[i], ((0, 0), (0, LANE - N_EXPERTS)))
            wg, wu, wd = moe_w_gate[i].astype(bf16), moe_w_up[i].astype(bf16), moe_w_down[i].astype(bf16)
            xl = _moe(h, router, wg, wu, wd, xl, gate_f, seq)

    out = _rmsnorm(xl, final_norm_w[None, :], out_dtype=jnp.float32)
    return out.reshape(bsz, seq, d)
```

```python
import functools
import math

import jax
import jax.numpy as jnp
import numpy as np
from jax import lax
from jax.experimental import pallas as pl
from jax.experimental.pallas import tpu as pltpu

D_MODEL = 4096
DEPTH = 2
GRID_W = 64

RET_WIDTH = D_MODEL // 4
RET_HEAD_DIM = 128
RET_HEADS = RET_WIDTH // RET_HEAD_DIM
SSD_WIDTH = D_MODEL // 2
SSD_HEAD_DIM = 64
SSD_HEADS = SSD_WIDTH // SSD_HEAD_DIM
SSD_GROUPS = 8
SSD_HEADS_PER_GROUP = SSD_HEADS // SSD_GROUPS
SSD_STATE = 128
SSD_CONV = 5
SSD_CONV_DIM = SSD_WIDTH + 2 * SSD_GROUPS * SSD_STATE
DIFF_WIDTH = D_MODEL // 4
DIFF_V_DIM = 128
DIFF_HEADS = DIFF_WIDTH // DIFF_V_DIM
DIFF_QK_DIM = DIFF_V_DIM // 2
MIX_WIDTH = RET_WIDTH + SSD_WIDTH + DIFF_WIDTH

IN_SIZES = (RET_WIDTH, RET_WIDTH, RET_WIDTH, RET_WIDTH,
            SSD_WIDTH, SSD_CONV_DIM, 2 * SSD_HEADS,
            2 * DIFF_HEADS * DIFF_QK_DIM, 2 * DIFF_HEADS * DIFF_QK_DIM, DIFF_WIDTH)
IN_WIDTH = sum(IN_SIZES)
IN_OFFS = tuple(int(s) for s in np.cumsum((0,) + IN_SIZES))
DT_WIDTH = 2 * SSD_HEADS
LANE = 128
DT_PAD = LANE
MAIN_WIDTH = IN_WIDTH - DT_WIDTH

COL_RET_Q, COL_RET_K, COL_RET_V, COL_RET_G = 0, 8, 16, 24
COL_SSD_Z = 32
COL_SSD_X, COL_SSD_B, COL_SSD_C = 48, 64, 72
COL_DIFF_Q, COL_DIFF_K, COL_DIFF_V = 80, 88, 96

KV_CHUNK = 256
SCAN_T = 256
GROUP_W = SSD_HEADS_PER_GROUP * SSD_HEAD_DIM
N_DIR_HEADS = 2 * SSD_HEADS_PER_GROUP

ROPE_BASE = 10000.0
LOG2_E = math.log2(math.e)
D_FF = 11008
N_EXPERTS = 8
TOP_K = 2
MOE_BLOCK = 512
NORM_EPS = 1e-6

VMEM_LIMIT = 56 * 1024 * 1024


def _cparams(sem):
    return pltpu.CompilerParams(dimension_semantics=sem, vmem_limit_bytes=VMEM_LIMIT)


def _silu(x):
    return x * jax.nn.sigmoid(x)


def _softplus(x):
    return jnp.maximum(x, 0.0) + jnp.log(1.0 + jnp.exp(-jnp.abs(x)))


def _f32_dot(a, b):
    return jnp.dot(a, b, preferred_element_type=jnp.float32, precision=lax.Precision.HIGHEST)


def _bf16_dot(a, b):
    return jnp.dot(a.astype(jnp.bfloat16), b.astype(jnp.bfloat16), preferred_element_type=jnp.float32)


def _mm_body(*refs, nk, n_rhs, has_res, grouped):
    refs = list(refs)
    if grouped:
        refs.pop(0)
    a_ref = refs.pop(0)
    b_refs = [refs.pop(0) for _ in range(n_rhs)]
    if has_res:
        res_ref = refs.pop(0)
        gate_ref = refs.pop(0)
    o_ref = refs.pop(0)
    acc_refs = refs

    def epilogue(accs):
        if n_rhs == 2:
            g, u = accs
            val = _silu(g) * u
        else:
            val = accs[0]
        if has_res:
            val = res_ref[...] + gate_ref[...] * val
        o_ref[...] = val.astype(o_ref.dtype)

    a = a_ref[...]
    if b_refs[0].dtype == jnp.float32:
        precision = lax.Precision.HIGHEST
    else:
        precision = None
        a = a.astype(b_refs[0].dtype)
    prods = [jnp.dot(a, b[...], preferred_element_type=jnp.float32, precision=precision) for b in b_refs]
    if nk == 1:
        epilogue(prods)
        return
    k = pl.program_id(2)

    @pl.when(k == 0)
    def _():
        for acc, p in zip(acc_refs, prods):
            acc[...] = p

    @pl.when(k > 0)
    def _():
        for acc, p in zip(acc_refs, prods):
            acc[...] += p

    @pl.when(k == nk - 1)
    def _():
        epilogue([acc[...] for acc in acc_refs])


def _matmul(a, bs, *, tm, tn, tk, out_dtype, res=None, gate=None, rows_per_gate=None,
            block_expert=None, n_major=False, name="matmul"):
    m, kdim = a.shape
    n = bs[0].shape[-1]
    nk = kdim // tk
    grouped = block_expert is not None
    n_rhs = len(bs)
    has_res = res is not None
    assert m % tm == 0 and n % tn == 0 and kdim % tk == 0

    if n_major:
        grid = (n // tn, m // tm, nk)
        ij = lambda g0, g1: (g1, g0)
    else:
        grid = (m // tm, n // tn, nk)
        ij = lambda g0, g1: (g0, g1)

    def a_map(g0, g1, k, *_):
        i, _j = ij(g0, g1)
        return (i, k)

    def b_map(g0, g1, k, *pref):
        i, j = ij(g0, g1)
        if grouped:
            return (pref[0][i], k, j)
        return (k, j)

    def o_map(g0, g1, k, *_):
        return ij(g0, g1)

    def gate_map(g0, g1, k, *_):
        i, j = ij(g0, g1)
        return ((i * tm) // rows_per_gate, 0, j)

    in_specs = [pl.BlockSpec((tm, tk), a_map)]
    b_block = (None, tk, tn) if grouped else (tk, tn)
    in_specs += [pl.BlockSpec(b_block, b_map) for _ in bs]
    operands = [a] + list(bs)
    if has_res:
        in_specs += [pl.BlockSpec((tm, tn), o_map), pl.BlockSpec((None, 1, tn), gate_map)]
        operands += [res, gate]
    scratch = [pltpu.VMEM((tm, tn), jnp.float32) for _ in bs] if nk > 1 else []
    body = functools.partial(_mm_body, nk=nk, n_rhs=n_rhs, has_res=has_res, grouped=grouped)
    grid_spec = pltpu.PrefetchScalarGridSpec(
        num_scalar_prefetch=1 if grouped else 0, grid=grid, in_specs=in_specs,
        out_specs=pl.BlockSpec((tm, tn), o_map), scratch_shapes=scratch)
    call = pl.pallas_call(
        body, grid_spec=grid_spec, out_shape=jax.ShapeDtypeStruct((m, n), out_dtype),
        compiler_params=_cparams(("parallel", "parallel", "arbitrary")), name=name)
    if grouped:
        return call(block_expert, *operands)
    return call(*operands)


def _ada_body(c_ref, w_ref, b_ref, o_ref):
    act = _silu(c_ref[...]).astype(jnp.bfloat16)
    o_ref[...] = jnp.dot(act, w_ref[...].astype(jnp.bfloat16),
                         preferred_element_type=jnp.float32) + b_ref[...]


def _ada_modulation(cond, w, b, layer):
    rows, d = cond.shape
    n = w.shape[2]
    tn = 512
    return pl.pallas_call(
        _ada_body, grid=(n // tn,),
        in_specs=[pl.BlockSpec((rows, d), lambda j: (0, 0)),
                  pl.BlockSpec((None, d, tn), lambda j: (layer, 0, j)),
                  pl.BlockSpec((None, 1, tn), lambda j: (layer, 0, j))],
        out_specs=pl.BlockSpec((rows, tn), lambda j: (0, j)),
        out_shape=jax.ShapeDtypeStruct((rows, n), jnp.float32),
        compiler_params=_cparams(("parallel",)), name="ada_modulation")(cond, w, b)


def _norm_body(*refs, modulated):
    if modulated:
        x_ref, w_ref, shift_ref, scale_ref, o_ref = refs
    else:
        x_ref, w_ref, o_ref = refs
    x = x_ref[...]
    y = x * lax.rsqrt(jnp.mean(x * x, axis=-1, keepdims=True) + NORM_EPS) * w_ref[...]
    if modulated:
        y = y * (1.0 + scale_ref[...]) + shift_ref[...]
    o_ref[...] = y.astype(o_ref.dtype)


def _rmsnorm(x, w, *, shift=None, scale=None, rows_per_vec=None, out_dtype, tm=256):
    m, d = x.shape
    modulated = shift is not None
    in_specs = [pl.BlockSpec((tm, d), lambda i: (i, 0)), pl.BlockSpec((1, d), lambda i: (0, 0))]
    operands = [x, w]
    if modulated:
        vec = pl.BlockSpec((None, 1, d), lambda i: ((i * tm) // rows_per_vec, 0, 0))
        in_specs += [vec, vec]
        operands += [shift, scale]
    return pl.pallas_call(
        functools.partial(_norm_body, modulated=modulated), grid=(m // tm,),
        in_specs=in_specs, out_specs=pl.BlockSpec((tm, d), lambda i: (i, 0)),
        out_shape=jax.ShapeDtypeStruct((m, d), out_dtype),
        compiler_params=_cparams(("parallel",)), name="rmsnorm")(*operands)


def _rope_tables(seq, head_dim, reps):
    rows = seq // GRID_W
    r = jnp.broadcast_to(jnp.arange(rows, dtype=jnp.float32)[:, None], (rows, GRID_W)).reshape(seq)
    col = jnp.broadcast_to(jnp.arange(GRID_W, dtype=jnp.float32)[None, :], (rows, GRID_W)).reshape(seq)
    n_freq = head_dim // 4
    inv_freq = ROPE_BASE ** (-jnp.arange(n_freq, dtype=jnp.float32) / n_freq)
    ang_r = r[:, None] * inv_freq
    ang_c = col[:, None] * inv_freq
    ang = jnp.concatenate([ang_r, ang_r, ang_c, ang_c], axis=-1)
    sign = jnp.concatenate([-jnp.ones(n_freq), jnp.ones(n_freq), -jnp.ones(n_freq), jnp.ones(n_freq)])
    return jnp.tile(jnp.cos(ang), (1, reps)), jnp.tile(jnp.sin(ang) * sign, (1, reps))


def _rope(t, cos, sin_signed, quarter):
    lane = lax.broadcasted_iota(jnp.int32, t.shape, 1)
    width = t.shape[1]
    partner = jnp.where(lane % (2 * quarter) < quarter,
                        pltpu.roll(t, width - quarter, 1), pltpu.roll(t, quarter, 1))
    return t * cos + partner * sin_signed


def _ssd_conv_body(x_ref, w_ref, b_ref, o_ref, *, seq, transpose, chunk):
    w = w_ref[...]
    bias = b_ref[...]
    half = SSD_CONV // 2
    for s in range(0, seq, chunk):
        lo, hi = max(s - 8, 0), min(s + chunk + 8, seq)
        xa = x_ref[lo:hi, :]
        n = hi - lo
        t_glob = lax.broadcasted_iota(jnp.int32, (n, 1), 0) + lo
        acc = xa * w[half:half + 1, :]
        for kk in range(SSD_CONV):
            d = kk - half
            if d == 0:
                continue
            shifted = pltpu.roll(xa, (-d) % n, 0)
            valid = jnp.logical_and(t_glob + d >= 0, t_glob + d < seq)
            acc = acc + jnp.where(valid, shifted, 0.0) * w[kk:kk + 1, :]
        y = _silu(acc + bias)[s - lo:s - lo + chunk, :]
        if transpose:
            o_ref[:, s:s + chunk] = y.T.astype(o_ref.dtype)
        else:
            o_ref[s:s + chunk, :] = y.astype(o_ref.dtype)


def _ssd_conv(proj, conv_w, conv_b, *, bsz, seq, col0, ncols, conv_col0, transpose, out_dtype):
    chunk = min(512, seq)
    if transpose:
        out_shape = jax.ShapeDtypeStruct((bsz, ncols * LANE, seq), out_dtype)
        out_spec = pl.BlockSpec((None, LANE, seq), lambda b, j: (b, j, 0))
    else:
        out_shape = jax.ShapeDtypeStruct((bsz * seq, ncols * LANE), out_dtype)
        out_spec = pl.BlockSpec((seq, LANE), lambda b, j: (b, j))
    return pl.pallas_call(
        functools.partial(_ssd_conv_body, seq=seq, transpose=transpose, chunk=chunk),
        grid=(bsz, ncols),
        in_specs=[pl.BlockSpec((seq, LANE), lambda b, j: (b, col0 + j)),
                  pl.BlockSpec((SSD_CONV, LANE), lambda b, j: (0, conv_col0 + j)),
                  pl.BlockSpec((1, LANE), lambda b, j: (0, conv_col0 + j))],
        out_specs=out_spec, out_shape=out_shape,
        compiler_params=_cparams(("parallel", "parallel")), name="ssd_conv")(proj, conv_w, conv_b)


def _upper_ones(n):
    r = lax.broadcasted_iota(jnp.int32, (n, n), 0)
    c = lax.broadcasted_iota(jnp.int32, (n, n), 1)
    return (r <= c).astype(jnp.float32)


def _head_lane_select(vals, width, shape):
    lane = lax.broadcasted_iota(jnp.int32, shape, len(shape) - 1)
    out = jnp.broadcast_to(vals[-1], shape)
    for r in range(len(vals) - 2, -1, -1):
        out = jnp.where(lane < (r + 1) * width, vals[r], out)
    return out


def _ssd_row_decay(dt_row_ref, bias_row_ref, aneg_row_ref):
    dt_r = _softplus(dt_row_ref[...] + bias_row_ref[...])
    a_r = dt_r * aneg_row_ref[...]
    cs_r = _f32_dot(a_r, _upper_ones(a_r.shape[1]))
    return dt_r, a_r, cs_r


def _ssd_bwd_state_body(bt_ref, x_ref, dt_row_ref, bias_row_ref, aneg_row_ref, s0_ref,
                        s_after_ref, s_fin_ref, s_scr, *, nc):
    k = pl.program_id(2)
    nh = SSD_HEADS_PER_GROUP

    @pl.when(k == 0)
    def _():
        s_scr[...] = s0_ref[...]

    s_after_ref[...] = s_scr[...]
    dt_r, a_r, cs_r = _ssd_row_decay(dt_row_ref, bias_row_ref, aneg_row_ref)
    t = a_r.shape[1]
    ecs_r = cs_r - a_r
    bt = bt_ref[...].astype(jnp.float32)
    xb = x_ref[...].astype(jnp.bfloat16)
    upd = []
    tot = []
    for r in range(nh):
        w_row = jnp.exp(ecs_r[nh + r:nh + r + 1, :]) * dt_r[nh + r:nh + r + 1, :]
        upd.append(_bf16_dot(bt * w_row, xb))
        tot.append(jnp.exp(cs_r[nh + r:nh + r + 1, t - 1:t]))
    shape = s_scr.shape
    s_new = _head_lane_select(tot, SSD_HEAD_DIM, shape) * s_scr[...] + _head_lane_select(upd, SSD_HEAD_DIM, shape)
    s_scr[...] = s_new

    @pl.when(k == nc - 1)
    def _():
        s_fin_ref[...] = s_new


def _ssd_bwd_state(bt, xs, dt_row, bias_row, aneg_row, s0, *, bsz, seq):
    nc = seq // SCAN_T
    g = SSD_GROUPS
    rev = lambda k: nc - 1 - k
    return pl.pallas_call(
        functools.partial(_ssd_bwd_state_body, nc=nc),
        grid=(bsz, g, nc),
        in_specs=[pl.BlockSpec((None, SSD_STATE, SCAN_T), lambda b, gi, k: (b, gi, rev(k))),
                  pl.BlockSpec((SCAN_T, GROUP_W), lambda b, gi, k: (b * nc + rev(k), gi)),
                  pl.BlockSpec((None, None, N_DIR_HEADS, SCAN_T), lambda b, gi, k: (b, gi, 0, rev(k))),
                  pl.BlockSpec((None, N_DIR_HEADS, 1), lambda b, gi, k: (gi, 0, 0)),
                  pl.BlockSpec((None, N_DIR_HEADS, 1), lambda b, gi, k: (gi, 0, 0)),
                  pl.BlockSpec((None, None, SSD_STATE, GROUP_W), lambda b, gi, k: (b, gi, 0, 0))],
        out_specs=[pl.BlockSpec((None, None, None, SSD_STATE, GROUP_W), lambda b, gi, k: (b, gi, rev(k), 0, 0)),
                   pl.BlockSpec((None, None, SSD_STATE, GROUP_W), lambda b, gi, k: (b, gi, 0, 0))],
        out_shape=[jax.ShapeDtypeStruct((bsz, g, nc, SSD_STATE, GROUP_W), jnp.float32),
                   jax.ShapeDtypeStruct((bsz, g, SSD_STATE, GROUP_W), jnp.float32)],
        scratch_shapes=[pltpu.VMEM((SSD_STATE, GROUP_W), jnp.float32)],
        compiler_params=_cparams(("parallel", "parallel", "arbitrary")),
        name="ssd_bwd_state")(bt, xs, dt_row, bias_row, aneg_row, s0)


def _ssd_fwd_body(c_ref, bt_ref, x_ref, z_ref, dt_row_ref, bias_row_ref, aneg_row_ref, dskip_ref, nw_ref,
                  s0_ref, s_after_ref, o_ref, s_fin_ref, s_scr, *, nc):
    k = pl.program_id(2)
    nh = SSD_HEADS_PER_GROUP

    @pl.when(k == 0)
    def _():
        s_scr[...] = s0_ref[...]

    dt_r, a_r, cs_r = _ssd_row_decay(dt_row_ref, bias_row_ref, aneg_row_ref)
    t = a_r.shape[1]
    ecs_r = cs_r - a_r
    n_vec = 2 * N_DIR_HEADS
    cols = jnp.concatenate([cs_r, ecs_r, jnp.zeros((LANE - n_vec, t), jnp.float32)], axis=0).T
    cs_c = cols[:, :N_DIR_HEADS]
    ecs_c = cols[:, N_DIR_HEADS:n_vec]

    cq = c_ref[...]
    bt = bt_ref[...]
    x = x_ref[...]
    xb = x.astype(jnp.bfloat16)
    scores = jnp.dot(cq, bt, preferred_element_type=jnp.float32)
    row = lax.broadcasted_iota(jnp.int32, (t, t), 0)
    col = lax.broadcasted_iota(jnp.int32, (t, t), 1)
    neg_inf = jnp.float32(-jnp.inf)
    btf = bt.astype(jnp.float32)

    intra, e_f, e_b, upd, tot = [], [], [], [], []
    for r in range(nh):
        fr, br = r, nh + r
        seg_f = jnp.where(row >= col, cs_c[:, fr:fr + 1] - cs_r[fr:fr + 1, :], neg_inf)
        seg_b = jnp.where(col >= row, ecs_r[br:br + 1, :] - ecs_c[:, br:br + 1], neg_inf)
        wgt = jnp.exp(seg_f) * dt_r[fr:fr + 1, :] + jnp.exp(seg_b) * dt_r[br:br + 1, :]
        intra.append(_bf16_dot(scores * wgt, xb))
        e_f.append(jnp.exp(cs_c[:, fr:fr + 1]))
        e_b.append(jnp.exp(cs_r[br:br + 1, t - 1:t] - ecs_c[:, br:br + 1]))
        tot_f = cs_r[fr:fr + 1, t - 1:t]
        w_row = jnp.exp(tot_f - cs_r[fr:fr + 1, :]) * dt_r[fr:fr + 1, :]
        upd.append(_bf16_dot(btf * w_row, xb))
        tot.append(jnp.exp(tot_f))

    shape = x.shape
    s_f = s_scr[...]
    y = _head_lane_select(intra, SSD_HEAD_DIM, shape)
    y = y + _head_lane_select(e_f, SSD_HEAD_DIM, shape) * _bf16_dot(cq, s_f)
    y = y + _head_lane_select(e_b, SSD_HEAD_DIM, shape) * _bf16_dot(cq, s_after_ref[...])
    y = (y + dskip_ref[...] * x) * _silu(z_ref[...])
    y = y * lax.rsqrt(jnp.mean(y * y, axis=-1, keepdims=True) + NORM_EPS) * nw_ref[...]
    o_ref[...] = y.astype(o_ref.dtype)

    s_new = (_head_lane_select(tot, SSD_HEAD_DIM, s_f.shape) * s_f
             + _head_lane_select(upd, SSD_HEAD_DIM, s_f.shape))
    s_scr[...] = s_new

    @pl.when(k == nc - 1)
    def _():
        s_fin_ref[...] = s_new


def _ssd_fwd(cm, bt, xs, proj, dt_row, bias_row, aneg_row, dskip, norm_w, s0, s_after, *, bsz, seq):
    nc = seq // SCAN_T
    g = SSD_GROUPS
    zcol = COL_SSD_Z * LANE // GROUP_W
    vec_row = pl.BlockSpec((None, N_DIR_HEADS, 1), lambda b, gi, k: (gi, 0, 0))
    return pl.pallas_call(
        functools.partial(_ssd_fwd_body, nc=nc),
        grid=(bsz, g, nc),
        in_specs=[pl.BlockSpec((SCAN_T, SSD_STATE), lambda b, gi, k: (b * nc + k, gi)),
                  pl.BlockSpec((None, SSD_STATE, SCAN_T), lambda b, gi, k: (b, gi, k)),
                  pl.BlockSpec((SCAN_T, GROUP_W), lambda b, gi, k: (b * nc + k, gi)),
                  pl.BlockSpec((SCAN_T, GROUP_W), lambda b, gi, k: (b * nc + k, zcol + gi)),
                  pl.BlockSpec((None, None, N_DIR_HEADS, SCAN_T), lambda b, gi, k: (b, gi, 0, k)),
                  vec_row, vec_row,
                  pl.BlockSpec((1, GROUP_W), lambda b, gi, k: (0, gi)),
                  pl.BlockSpec((1, GROUP_W), lambda b, gi, k: (0, gi)),
                  pl.BlockSpec((None, None, SSD_STATE, GROUP_W), lambda b, gi, k: (b, gi, 0, 0)),
                  pl.BlockSpec((None, None, None, SSD_STATE, GROUP_W), lambda b, gi, k: (b, gi, k, 0, 0))],
        out_specs=[pl.BlockSpec((SCAN_T, GROUP_W), lambda b, gi, k: (b * nc + k, gi)),
                   pl.BlockSpec((None, None, SSD_STATE, GROUP_W), lambda b, gi, k: (b, gi, 0, 0))],
        out_shape=[jax.ShapeDtypeStruct((bsz * seq, SSD_WIDTH), jnp.bfloat16),
                   jax.ShapeDtypeStruct((bsz, g, SSD_STATE, GROUP_W), jnp.float32)],
        scratch_shapes=[pltpu.VMEM((SSD_STATE, GROUP_W), jnp.float32)],
        compiler_params=_cparams(("parallel", "parallel", "arbitrary")), name="ssd_fwd")(
            cm, bt, xs, proj, dt_row, bias_row, aneg_row, dskip, norm_w, s0, s_after)


def _ssd_group(proj, proj_dt, proj_c, proj_c_dt, lw, *, bsz, seq, n_ctx, need_ctx_out):
    g, nh = SSD_GROUPS, SSD_HEADS_PER_GROUP
    conv_w = lw['ssd_conv_w']
    conv_b = lw['ssd_conv_b'][None, :]

    def per_group(v):
        return v.astype(jnp.float32).reshape(2, g, nh).transpose(1, 0, 2).reshape(g, 2 * nh)

    bias = per_group(lw['ssd_dt_bias'])
    aneg = per_group(-jnp.exp(lw['ssd_a_log'].astype(jnp.float32)))
    bias_row, aneg_row = bias[:, :, None], aneg[:, :, None]
    dskip = jnp.repeat(lw['ssd_d'].astype(jnp.float32), SSD_HEAD_DIM)[None, :]
    norm_w = lw['ssd_norm_w'][None, :]

    def prepare(p, p_dt, n):
        conv = functools.partial(_ssd_conv, p, conv_w, conv_b, bsz=bsz, seq=n)
        xs = conv(col0=COL_SSD_X, ncols=SSD_WIDTH // LANE, conv_col0=0, transpose=False, out_dtype=jnp.float32)
        bt = conv(col0=COL_SSD_B, ncols=g, conv_col0=SSD_WIDTH // LANE, transpose=True, out_dtype=jnp.bfloat16)
        cm = conv(col0=COL_SSD_C, ncols=g, conv_col0=SSD_WIDTH // LANE + g, transpose=False,
                  out_dtype=jnp.bfloat16)
        dt_row = p_dt[:, :DT_WIDTH].reshape(bsz, n, 2, g, nh).transpose(0, 3, 2, 4, 1).reshape(bsz, g, 2 * nh, n)
        return xs, bt, cm, dt_row

    zeros = jnp.zeros((bsz, g, SSD_STATE, GROUP_W), jnp.float32)
    xs_c, bt_c, cm_c, dtr_c = prepare(proj_c, proj_c_dt, n_ctx)
    xs, bt, cm, dt_row = prepare(proj, proj_dt, seq)
    sa_c, sb0 = _ssd_bwd_state(bt_c, xs_c, dtr_c, bias_row, aneg_row, zeros, bsz=bsz, seq=n_ctx)
    fwd = functools.partial(_ssd_fwd, bias_row=bias_row, aneg_row=aneg_row, dskip=dskip, norm_w=norm_w, bsz=bsz)
    y_c, sf0 = fwd(cm_c, bt_c, xs_c, proj_c, dtr_c, s0=zeros, s_after=sa_c, seq=n_ctx)
    sa, _ = _ssd_bwd_state(bt, xs, dt_row, bias_row, aneg_row, sb0, bsz=bsz, seq=seq)
    y, _ = fwd(cm, bt, xs, proj, dt_row, s0=sf0, s_after=sa, seq=seq)
    return y, (y_c if need_ctx_out else None)


def _ret_prep_body(q_ref, k_ref, cos_ref, sin_ref, qo_ref, kto_ref, *, use_rope):
    q = q_ref[...]
    kk = k_ref[...]
    if use_rope:
        q = _rope(q, cos_ref[...], sin_ref[...], RET_HEAD_DIM // 4)
        kk = _rope(kk, cos_ref[...], sin_ref[...], RET_HEAD_DIM // 4)
    kk = kk * (RET_HEAD_DIM ** -0.5)
    qo_ref[...] = q.astype(qo_ref.dtype)
    kto_ref[...] = kk.T.astype(kto_ref.dtype)


def _ret_prep(proj, cos, sin, *, bsz, seq, use_rope):
    tr = min(512, seq)
    nr = seq // tr
    return pl.pallas_call(
        functools.partial(_ret_prep_body, use_rope=use_rope),
        grid=(bsz, RET_HEADS, nr),
        in_specs=[pl.BlockSpec((tr, LANE), lambda b, h, i: (b * nr + i, COL_RET_Q + h)),
                  pl.BlockSpec((tr, LANE), lambda b, h, i: (b * nr + i, COL_RET_K + h)),
                  pl.BlockSpec((tr, LANE), lambda b, h, i: (i, 0)),
                  pl.BlockSpec((tr, LANE), lambda b, h, i: (i, 0))],
        out_specs=[pl.BlockSpec((tr, LANE), lambda b, h, i: (b * nr + i, h)),
                   pl.BlockSpec((None, LANE, tr), lambda b, h, i: (b, h, i))],
        out_shape=[jax.ShapeDtypeStruct((bsz * seq, RET_WIDTH), jnp.bfloat16),
                   jax.ShapeDtypeStruct((bsz, RET_WIDTH, seq), jnp.bfloat16)],
        compiler_params=_cparams(("parallel", "parallel", "parallel")), name="ret_prep")(proj, proj, cos, sin)


def _ret_bwd_state_body(lg_ref, kt_ref, v_ref, s0_ref, s_after_ref, s_fin_ref, s_scr, *, nc):
    h = pl.program_id(1)
    k = pl.program_id(2)

    @pl.when(k == 0)
    def _():
        s_scr[...] = s0_ref[...]

    s_after_ref[...] = s_scr[...]
    lgb = jnp.full((1, 1), lg_ref[1, h], jnp.float32)
    t = kt_ref.shape[1]
    j = lax.broadcasted_iota(jnp.int32, (1, t), 1).astype(jnp.float32)
    kt = kt_ref[...].astype(jnp.float32) * jnp.exp(j * lgb)
    s_new = jnp.exp(t * lgb) * s_scr[...] + _bf16_dot(kt, v_ref[...])
    s_scr[...] = s_new

    @pl.when(k == nc - 1)
    def _():
        s_fin_ref[...] = s_new


def _ret_bwd_state(lg, kt, proj, s0, *, bsz, seq):
    nc = seq // SCAN_T
    rev = lambda k: nc - 1 - k
    grid_spec = pltpu.PrefetchScalarGridSpec(
        num_scalar_prefetch=0, grid=(bsz, RET_HEADS, nc),
        in_specs=[pl.BlockSpec(memory_space=pltpu.SMEM),
                  pl.BlockSpec((None, LANE, SCAN_T), lambda b, h, k: (b, h, rev(k))),
                  pl.BlockSpec((SCAN_T, LANE), lambda b, h, k: (b * nc + rev(k), COL_RET_V + h)),
                  pl.BlockSpec((None, None, LANE, LANE), lambda b, h, k: (b, h, 0, 0))],
        out_specs=[pl.BlockSpec((None, None, None, LANE, LANE), lambda b, h, k: (b, h, rev(k), 0, 0)),
                   pl.BlockSpec((None, None, LANE, LANE), lambda b, h, k: (b, h, 0, 0))],
        scratch_shapes=[pltpu.VMEM((LANE, LANE), jnp.float32)])
    return pl.pallas_call(
        functools.partial(_ret_bwd_state_body, nc=nc), grid_spec=grid_spec,
        out_shape=[jax.ShapeDtypeStruct((bsz, RET_HEADS, nc, LANE, LANE), jnp.float32),
                   jax.ShapeDtypeStruct((bsz, RET_HEADS, LANE, LANE), jnp.float32)],
        compiler_params=_cparams(("parallel", "parallel", "arbitrary")), name="ret_bwd_state")(lg, kt, proj, s0)


def _ret_fwd_body(lg_ref, q_ref, kt_ref, v_ref, g_ref, nw_ref, s0_ref, s_after_ref,
                  o_ref, s_fin_ref, s_scr, d_scr, *, nc):
    h = pl.program_id(1)
    k = pl.program_id(2)
    lgf = jnp.full((1, 1), lg_ref[0, h], jnp.float32)
    lgb = jnp.full((1, 1), lg_ref[1, h], jnp.float32)
    t = q_ref.shape[0]

    @pl.when(k == 0)
    def _():
        s_scr[...] = s0_ref[...]
        row = lax.broadcasted_iota(jnp.int32, (t, t), 0)
        col = lax.broadcasted_iota(jnp.int32, (t, t), 1)
        dist = (row - col).astype(jnp.float32)
        neg_inf = jnp.float32(-jnp.inf)
        d_scr[...] = (jnp.exp(jnp.where(row >= col, dist * lgf, neg_inf))
                      + jnp.exp(jnp.where(col >= row, -dist * lgb, neg_inf)))

    q = q_ref[...]
    kt = kt_ref[...]
    v = v_ref[...].astype(jnp.bfloat16)
    scores = jnp.dot(q, kt, preferred_element_type=jnp.float32)
    i = lax.broadcasted_iota(jnp.int32, (t, 1), 0).astype(jnp.float32)
    j = lax.broadcasted_iota(jnp.int32, (1, t), 1).astype(jnp.float32)
    s_f = s_scr[...]
    y = _bf16_dot(scores * d_scr[...], v)
    y = y + jnp.exp((i + 1.0) * lgf) * _bf16_dot(q, s_f)
    y = y + jnp.exp((t - i) * lgb) * _bf16_dot(q, s_after_ref[...])
    y = y * lax.rsqrt(jnp.mean(y * y, axis=-1, keepdims=True) + NORM_EPS) * nw_ref[...]
    o_ref[...] = (_silu(g_ref[...]) * y).astype(o_ref.dtype)

    ktw = kt.astype(jnp.float32) * jnp.exp((t - 1.0 - j) * lgf)
    s_new = jnp.exp(t * lgf) * s_f + _bf16_dot(ktw, v)
    s_scr[...] = s_new

    @pl.when(k == nc - 1)
    def _():
        s_fin_ref[...] = s_new


def _ret_fwd(lg, q, kt, proj, norm_w, s0, s_after, *, bsz, seq):
    nc = seq // SCAN_T
    grid_spec = pltpu.PrefetchScalarGridSpec(
        num_scalar_prefetch=0, grid=(bsz, RET_HEADS, nc),
        in_specs=[pl.BlockSpec(memory_space=pltpu.SMEM),
                  pl.BlockSpec((SCAN_T, LANE), lambda b, h, k: (b * nc + k, h)),
                  pl.BlockSpec((None, LANE, SCAN_T), lambda b, h, k: (b, h, k)),
                  pl.BlockSpec((SCAN_T, LANE), lambda b, h, k: (b * nc + k, COL_RET_V + h)),
                  pl.BlockSpec((SCAN_T, LANE), lambda b, h, k: (b * nc + k, COL_RET_G + h)),
                  pl.BlockSpec((1, LANE), lambda b, h, k: (0, h)),
                  pl.BlockSpec((None, None, LANE, LANE), lambda b, h, k: (b, h, 0, 0)),
                  pl.BlockSpec((None, None, None, LANE, LANE), lambda b, h, k: (b, h, k, 0, 0))],
        out_specs=[pl.BlockSpec((SCAN_T, LANE), lambda b, h, k: (b * nc + k, h)),
                   pl.BlockSpec((None, None, LANE, LANE), lambda b, h, k: (b, h, 0, 0))],
        scratch_shapes=[pltpu.VMEM((LANE, LANE), jnp.float32), pltpu.VMEM((SCAN_T, SCAN_T), jnp.float32)])
    return pl.pallas_call(
        functools.partial(_ret_fwd_body, nc=nc), grid_spec=grid_spec,
        out_shape=[jax.ShapeDtypeStruct((bsz * seq, RET_WIDTH), jnp.bfloat16),
                   jax.ShapeDtypeStruct((bsz, RET_HEADS, LANE, LANE), jnp.float32)],
        compiler_params=_cparams(("parallel", "parallel", "arbitrary")), name="ret_fwd")(
            lg, q, kt, proj, proj, norm_w, s0, s_after)


def _ret_group(proj, proj_c, lw, rope, *, bsz, seq, n_ctx, need_ctx_out):
    cos, sin = rope
    lg = jax.nn.log_sigmoid(lw['ret_decay_logit'].astype(jnp.float32))
    norm_w = lw['ret_norm_w'][None, :]
    zeros = jnp.zeros((bsz, RET_HEADS, LANE, LANE), jnp.float32)
    q_c, kt_c = _ret_prep(proj_c, cos, sin, bsz=bsz, seq=n_ctx, use_rope=False)
    q, kt = _ret_prep(proj, cos, sin, bsz=bsz, seq=seq, use_rope=True)
    sa_c, sb0 = _ret_bwd_state(lg, kt_c, proj_c, zeros, bsz=bsz, seq=n_ctx)
    y_c, sf0 = _ret_fwd(lg, q_c, kt_c, proj_c, norm_w, zeros, sa_c, bsz=bsz, seq=n_ctx)
    sa, _ = _ret_bwd_state(lg, kt, proj, sb0, bsz=bsz, seq=seq)
    y, _ = _ret_fwd(lg, q, kt, proj, norm_w, sf0, sa, bsz=bsz, seq=seq)
    return y, (y_c if need_ctx_out else None)


def _diff_prep_body(q_ref, k_ref, v_ref, cos_ref, sin_ref, qt1_ref, qt2_ref, ko_ref, vto_ref, *, use_rope):
    q = q_ref[...]
    kk = k_ref[...]
    if use_rope:
        q = _rope(q, cos_ref[...], sin_ref[...], DIFF_QK_DIM // 4)
        kk = _rope(kk, cos_ref[...], sin_ref[...], DIFF_QK_DIM // 4)
    qt = (q * (DIFF_QK_DIM ** -0.5 * LOG2_E)).T
    sub = lax.broadcasted_iota(jnp.int32, qt.shape, 0)
    qt1_ref[...] = jnp.where(sub < DIFF_QK_DIM, qt, 0.0).astype(qt1_ref.dtype)
    qt2_ref[...] = jnp.where(sub >= DIFF_QK_DIM, qt, 0.0).astype(qt2_ref.dtype)
    ko_ref[...] = kk.astype(ko_ref.dtype)
    vto_ref[...] = v_ref[...].T.astype(vto_ref.dtype)


def _diff_prep(proj, cos, sin, *, bsz, seq, use_rope):
    tr = KV_CHUNK
    nr = seq // tr
    t_spec = pl.BlockSpec((None, LANE, tr), lambda b, h, i: (b, h, i))
    t_shape = jax.ShapeDtypeStruct((bsz, DIFF_WIDTH, seq), jnp.bfloat16)
    return pl.pallas_call(
        functools.partial(_diff_prep_body, use_rope=use_rope),
        grid=(bsz, DIFF_HEADS, nr),
        in_specs=[pl.BlockSpec((tr, LANE), lambda b, h, i: (b * nr + i, COL_DIFF_Q + h)),
                  pl.BlockSpec((tr, LANE), lambda b, h, i: (b * nr + i, COL_DIFF_K + h)),
                  pl.BlockSpec((tr, LANE), lambda b, h, i: (b * nr + i, COL_DIFF_V + h)),
                  pl.BlockSpec((tr, LANE), lambda b, h, i: (i, 0)),
                  pl.BlockSpec((tr, LANE), lambda b, h, i: (i, 0))],
        out_specs=[t_spec, t_spec,
                   pl.BlockSpec((None, tr, LANE), lambda b, h, i: (b, i, h)),
                   pl.BlockSpec((None, None, LANE, tr), lambda b, h, i: (b, i, h, 0))],
        out_shape=[t_shape, t_shape,
                   jax.ShapeDtypeStruct((bsz, seq, DIFF_WIDTH), jnp.bfloat16),
                   jax.ShapeDtypeStruct((bsz, nr, DIFF_WIDTH, tr), jnp.bfloat16)],
        compiler_params=_cparams(("parallel", "parallel", "parallel")), name="diff_prep")(
            proj, proj, proj, cos, sin)


def _diff_attn_body(lam_ref, qt1_ref, qt2_ref, k_ref, vt_ref, nw_ref, o_ref, acc1, acc2, s_scr, *,
                    chunks, out_scale):
    tk = chunks * KV_CHUNK
    n_steps = k_ref.shape[0] // tk
    tq = qt1_ref.shape[1]
    qts = (qt1_ref[...], qt2_ref[...])
    accs = (acc1, acc2)
    for acc in accs:
        acc[...] = jnp.zeros_like(acc)

    def scores(step, slot):
        start = pl.multiple_of(step * tk, tk)
        kk = k_ref[pl.ds(start, tk), :]
        for mp in range(2):
            s_scr[slot, mp] = jnp.dot(kk, qts[mp], preferred_element_type=jnp.float32)

    def consume(step, slot, carry):
        new = []
        for mp in range(2):
            m, l = carry[mp]
            s = s_scr[slot, mp]
            m_new = jnp.maximum(m, jnp.max(s, axis=0, keepdims=True))
            alpha = jnp.exp2(m - m_new)
            p = jnp.exp2(s - m_new)
            l_new = alpha * l + jnp.sum(p, axis=0, keepdims=True)
            pb = p.astype(jnp.bfloat16)
            pv = jnp.dot(vt_ref[step * chunks], pb[:KV_CHUNK], preferred_element_type=jnp.float32)
            for c in range(1, chunks):
                pv = pv + jnp.dot(vt_ref[step * chunks + c], pb[c * KV_CHUNK:(c + 1) * KV_CHUNK],
                                  preferred_element_type=jnp.float32)
            accs[mp][...] = alpha * accs[mp][...] + pv
            new.append((m_new, l_new))
        return tuple(new)

    def pair(jj, carry):
        j = 2 * jj
        scores(j + 1, 1)
        carry = consume(j, 0, carry)
        scores(jnp.minimum(j + 2, n_steps - 1), 0)
        return consume(j + 1, 1, carry)

    init = tuple((jnp.full((1, tq), -jnp.inf, jnp.float32), jnp.zeros((1, tq), jnp.float32)) for _ in range(2))
    scores(0, 0)
    carry = lax.fori_loop(0, n_steps // 2, pair, init)
    if n_steps % 2:
        carry = consume(n_steps - 1, 0, carry)
    (m1, l1), (m2, l2) = carry
    o = acc1[...] / l1 - lam_ref[0] * (acc2[...] / l2)
    o = o * lax.rsqrt(jnp.mean(o * o, axis=0, keepdims=True) + NORM_EPS)
    o_ref[...] = (o.T * nw_ref[...] * out_scale).astype(o_ref.dtype)


def _diff_attn(lam, qt1, qt2, k_all, vt_all, norm_w, *, bsz, n_q, lambda_init, tq):
    nq = n_q // tq
    kv_len = k_all.shape[1]
    n_chunks = kv_len // KV_CHUNK
    chunks = 3 if n_chunks % 3 == 0 else 1
    grid_spec = pltpu.PrefetchScalarGridSpec(
        num_scalar_prefetch=0, grid=(bsz, DIFF_HEADS, nq),
        in_specs=[pl.BlockSpec(memory_space=pltpu.SMEM),
                  pl.BlockSpec((None, LANE, tq), lambda b, h, i: (b, h, i)),
                  pl.BlockSpec((None, LANE, tq), lambda b, h, i: (b, h, i)),
                  pl.BlockSpec((None, kv_len, LANE), lambda b, h, i: (b, 0, h)),
                  pl.BlockSpec((None, n_chunks, LANE, KV_CHUNK), lambda b, h, i: (b, 0, h, 0)),
                  pl.BlockSpec((1, LANE), lambda b, h, i: (0, 0))],
        out_specs=pl.BlockSpec((tq, LANE), lambda b, h, i: (b * nq + i, h)),
        scratch_shapes=[pltpu.VMEM((LANE, tq), jnp.float32), pltpu.VMEM((LANE, tq), jnp.float32),
                        pltpu.VMEM((2, 2, chunks * KV_CHUNK, tq), jnp.float32)])
    return pl.pallas_call(
        functools.partial(_diff_attn_body, chunks=chunks, out_scale=1.0 - lambda_init), grid_spec=grid_spec,
        out_shape=jax.ShapeDtypeStruct((bsz * n_q, DIFF_WIDTH), jnp.bfloat16),
        compiler_params=_cparams(("parallel", "parallel", "arbitrary")), name="diff_attn")(
            lam, qt1, qt2, k_all, vt_all, norm_w)


def _diff_group(proj, proj_c, lw, rope, *, bsz, seq, n_ctx, lambda_init, need_ctx_out):
    cos, sin = rope
    lv = lw['diff_lambda'].astype(jnp.float32)
    lam = (jnp.exp(jnp.sum(lv[0] * lv[1])) - jnp.exp(jnp.sum(lv[2] * lv[3])) + lambda_init).reshape(1)
    norm_w = lw['diff_norm_w'][None, :]
    qt1, qt2, k_l, vt_l = _diff_prep(proj, cos, sin, bsz=bsz, seq=seq, use_rope=True)
    qt1c, qt2c, k_c, vt_c = _diff_prep(proj_c, cos, sin, bsz=bsz, seq=n_ctx, use_rope=False)
    k_all = jnp.concatenate([k_l, k_c], axis=1)
    vt_all = jnp.concatenate([vt_l, vt_c], axis=1)
    y = _diff_attn(lam, qt1, qt2, k_all, vt_all, norm_w, bsz=bsz, n_q=seq, lambda_init=lambda_init,
                   tq=min(1024, seq))
    y_c = None
    if need_ctx_out:
        y_c = _diff_attn(lam, qt1c, qt2c, k_c, vt_c, norm_w, bsz=bsz, n_q=n_ctx, lambda_init=lambda_init,
                         tq=n_ctx)
    return y, y_c


def _mixers(proj, proj_dt, proj_c, proj_c_dt, lw, *, bsz, seq, n_ctx, lambda_init, need_ctx_out):
    rope_ret = _rope_tables(seq, RET_HEAD_DIM, 1)
    rope_diff = _rope_tables(seq, DIFF_QK_DIM, 2)
    r_l, r_c = _ret_group(proj, proj_c, lw, rope_ret, bsz=bsz, seq=seq, n_ctx=n_ctx, need_ctx_out=need_ctx_out)
    s_l, s_c = _ssd_group(proj, proj_dt, proj_c, proj_c_dt, lw, bsz=bsz, seq=seq, n_ctx=n_ctx,
                          need_ctx_out=need_ctx_out)
    d_l, d_c = _diff_group(proj, proj_c, lw, rope_diff, bsz=bsz, seq=seq, n_ctx=n_ctx,
                           lambda_init=lambda_init, need_ctx_out=need_ctx_out)
    mix = jnp.concatenate([r_l, s_l, d_l], axis=-1)
    mix_c = jnp.concatenate([r_c, s_c, d_c], axis=-1) if need_ctx_out else None
    return mix, mix_c


def _gather_rows_body(idx_ref, src_hbm, o_ref, sem, *, tm):
    base = pl.program_id(0) * tm

    def row_copy(r):
        return pltpu.make_async_copy(src_hbm.at[pl.ds(idx_ref[base + r], 1)], o_ref.at[pl.ds(r, 1)], sem)

    def start(r, carry):
        row_copy(r).start()
        return carry

    def wait(r, carry):
        row_copy(r).wait()
        return carry

    lax.fori_loop(0, tm, start, 0)
    lax.fori_loop(0, tm, wait, 0)


def _gather_rows(idx, src, *, tm):
    n_rows = idx.shape[0]
    d = src.shape[1]
    grid_spec = pltpu.PrefetchScalarGridSpec(
        num_scalar_prefetch=1, grid=(n_rows // tm,),
        in_specs=[pl.BlockSpec(memory_space=pl.ANY)],
        out_specs=pl.BlockSpec((tm, d), lambda i, idx_ref: (i, 0)),
        scratch_shapes=[pltpu.SemaphoreType.DMA(())])
    return pl.pallas_call(
        functools.partial(_gather_rows_body, tm=tm), grid_spec=grid_spec,
        out_shape=jax.ShapeDtypeStruct((n_rows, d), src.dtype),
        compiler_params=_cparams(("arbitrary",)), name="moe_gather_rows")(idx, src)


def _moe_combine_body(pos_ref, y_hbm, g_ref, res_ref, gate_ref, o_ref, buf, sem, *, tm, n_steps):
    i = pl.program_id(0)

    def row_copy(step, slot, r, kk):
        src_row = pos_ref[(step * tm + r) * TOP_K + kk]
        return pltpu.make_async_copy(y_hbm.at[pl.ds(src_row, 1)], buf.at[slot, kk, pl.ds(r, 1)], sem.at[slot])

    def start_block(step, slot):
        def body(r, carry):
            for kk in range(TOP_K):
                row_copy(step, slot, r, kk).start()
            return carry
        lax.fori_loop(0, tm, body, 0)

    def wait_block(step, slot):
        def body(r, carry):
            for kk in range(TOP_K):
                row_copy(step, slot, r, kk).wait()
            return carry
        lax.fori_loop(0, tm, body, 0)

    slot = i % 2

    @pl.when(i == 0)
    def _():
        start_block(0, 0)

    @pl.when(i + 1 < n_steps)
    def _():
        start_block(i + 1, 1 - slot)

    wait_block(i, slot)
    g = g_ref[...]
    f = g[:, 0:1] * buf[slot, 0]
    for kk in range(1, TOP_K):
        f = f + g[:, kk:kk + 1] * buf[slot, kk]
    o_ref[...] = res_ref[...] + gate_ref[...] * f


def _moe_combine(pos, y_rows, gates, res, gate_vec, rows_per_gate, *, tm=128):
    n_tok, d = res.shape
    n_steps = n_tok // tm
    grid_spec = pltpu.PrefetchScalarGridSpec(
        num_scalar_prefetch=1, grid=(n_steps,),
        in_specs=[pl.BlockSpec(memory_space=pl.ANY),
                  pl.BlockSpec((tm, TOP_K), lambda i, p: (i, 0)),
                  pl.BlockSpec((tm, d), lambda i, p: (i, 0)),
                  pl.BlockSpec((None, 1, d), lambda i, p: ((i * tm) // rows_per_gate, 0, 0))],
        out_specs=pl.BlockSpec((tm, d), lambda i, p: (i, 0)),
        scratch_shapes=[pltpu.VMEM((2, TOP_K, tm, d), jnp.float32), pltpu.SemaphoreType.DMA((2,))])
    return pl.pallas_call(
        functools.partial(_moe_combine_body, tm=tm, n_steps=n_steps), grid_spec=grid_spec,
        out_shape=jax.ShapeDtypeStruct((n_tok, d), jnp.float32),
        compiler_params=_cparams(("arbitrary",)), name="moe_combine")(pos, y_rows, gates, res, gate_vec)


def _moe(h, router_w, w_gate, w_up, w_down, res, gate_vec, rows_per_gate):
    n_tok, d = h.shape
    n_exp = N_EXPERTS
    n_assign = n_tok * TOP_K
    logits = _matmul(h, [router_w], tm=512, tn=LANE, tk=d, out_dtype=jnp.float32, name="moe_router")[:, :n_exp]
    top_logits, top_idx = lax.top_k(logits, TOP_K)
    gates = jax.nn.softmax(top_logits, axis=-1)
    expert = top_idx.reshape(n_assign)
    order = jnp.argsort(expert)
    expert_s = expert[order]
    token_s = order // TOP_K
    counts = jnp.sum(expert[:, None] == jnp.arange(n_exp)[None, :], axis=0, dtype=jnp.int32)
    starts = jnp.cumsum(counts) - counts
    padded = (counts + MOE_BLOCK - 1) // MOE_BLOCK * MOE_BLOCK
    pad_ends = jnp.cumsum(padded)
    pad_starts = pad_ends - padded
    dest = pad_starts[expert_s] + jnp.arange(n_assign) - starts[expert_s]
    n_blocks = -(-n_assign // MOE_BLOCK) + n_exp
    block_expert = jnp.minimum(
        jnp.searchsorted(pad_ends, jnp.arange(n_blocks) * MOE_BLOCK, side='right'), n_exp - 1
    ).astype(jnp.int32)
    slot = jnp.arange(n_blocks * MOE_BLOCK)
    slot_expert = block_expert[slot // MOE_BLOCK]
    slot_off = slot - pad_starts[slot_expert]
    slot_src = jnp.clip(starts[slot_expert] + slot_off, 0, n_assign - 1)
    src_tok = jnp.where(slot_off < counts[slot_expert], token_s[slot_src], 0).astype(jnp.int32)
    rows = _gather_rows(src_tok, h, tm=MOE_BLOCK // 2)
    t = _matmul(rows, [w_gate, w_up], tm=MOE_BLOCK, tn=512, tk=d, out_dtype=jnp.bfloat16,
                block_expert=block_expert, n_major=True, name="moe_gate_up")
    y_rows = _matmul(t, [w_down], tm=MOE_BLOCK, tn=1024, tk=w_down.shape[1], out_dtype=jnp.float32,
                     block_expert=block_expert, n_major=True, name="moe_down")
    pos = dest[jnp.argsort(order)].astype(jnp.int32)
    return _moe_combine(pos, y_rows, gates, res, gate_vec, rows_per_gate)


def kernel(x, c, ctx, c_ctx, ada_w, ada_b, norm_mix_w, norm_ffn_w, w_in, w_out, ret_decay_logit, ret_norm_w, ssd_conv_w, ssd_conv_b, ssd_dt_bias, ssd_a_log, ssd_d, ssd_norm_w, diff_lambda, diff_norm_w, dense_w_gate, dense_w_up, dense_w_down, moe_router, moe_w_gate, moe_w_up, moe_w_down, final_norm_w):
    bsz, seq, d = x.shape
    n_ctx = ctx.shape[1]
    assert n_ctx % SCAN_T == 0 and n_ctx % KV_CHUNK == 0 and seq % 1024 == 0
    bf16 = jnp.bfloat16

    xl = x.reshape(bsz * seq, d)
    xc = ctx.reshape(bsz * n_ctx, d)
    cond = jnp.concatenate([c, c_ctx[None, :], jnp.zeros((8 - bsz - 1, d), jnp.float32)], axis=0)

    for layer in range(DEPTH):
        last = layer == DEPTH - 1
        lambda_init = 0.8 - 0.6 * math.exp(-0.3 * layer)
        lw = {'ret_decay_logit': ret_decay_logit[layer], 'ret_norm_w': ret_norm_w[layer],
              'ssd_conv_w': ssd_conv_w[layer], 'ssd_conv_b': ssd_conv_b[layer],
              'ssd_dt_bias': ssd_dt_bias[layer], 'ssd_a_log': ssd_a_log[layer], 'ssd_d': ssd_d[layer],
              'ssd_norm_w': ssd_norm_w[layer], 'diff_lambda': diff_lambda[layer],
              'diff_norm_w': diff_norm_w[layer]}

        mod = _ada_modulation(cond, ada_w, ada_b[:, None, :], layer)
        mod_l = mod[:bsz].reshape(bsz, 1, 6, d)
        mod_c = mod[bsz:bsz + 1].reshape(1, 1, 6, d)
        shift_m, scale_m, gate_m, shift_f, scale_f, gate_f = [mod_l[:, :, i] for i in range(6)]
        shift_mc, scale_mc, gate_mc, shift_fc, scale_fc, gate_fc = [mod_c[:, :, i] for i in range(6)]

        w_in_l = w_in[layer]
        w_main = jnp.concatenate([w_in_l[:, :IN_OFFS[6]], w_in_l[:, IN_OFFS[7]:]], axis=1).astype(bf16)
        w_dt = jnp.pad(w_in_l[:, IN_OFFS[6]:IN_OFFS[7]], ((0, 0), (0, DT_PAD - DT_WIDTH))).astype(bf16)
        w_out_l = w_out[layer].astype(bf16)
        nmw = norm_mix_w[layer][None, :]
        nfw = norm_ffn_w[layer][None, :]

        h = _rmsnorm(xl, nmw, shift=shift_m, scale=scale_m, rows_per_vec=seq, out_dtype=bf16)
        hc = _rmsnorm(xc, nmw, shift=shift_mc, scale=scale_mc, rows_per_vec=bsz * n_ctx, out_dtype=bf16)
        proj = _matmul(h, [w_main], tm=1024, tn=1024, tk=d, out_dtype=jnp.float32, name="in_proj")
        proj_dt = _matmul(h, [w_dt], tm=1024, tn=DT_PAD, tk=d, out_dtype=jnp.float32, name="in_proj_dt")
        proj_c = _matmul(hc, [w_main], tm=512, tn=1024, tk=d, out_dtype=jnp.float32, name="in_proj_ctx")
        proj_c_dt = _matmul(hc, [w_dt], tm=512, tn=DT_PAD, tk=d, out_dtype=jnp.float32, name="in_proj_dt_ctx")
        mix, mix_c = _mixers(proj, proj_dt, proj_c, proj_c_dt, lw, bsz=bsz, seq=seq, n_ctx=n_ctx,
                             lambda_init=lambda_init, need_ctx_out=not last)
        xl = _matmul(mix, [w_out_l], tm=512, tn=1024, tk=MIX_WIDTH, out_dtype=jnp.float32,
                     res=xl, gate=gate_m, rows_per_gate=seq, name="out_proj")
        if not last:
            xc = _matmul(mix_c, [w_out_l], tm=512, tn=1024, tk=MIX_WIDTH // 2, out_dtype=jnp.float32,
                         res=xc, gate=gate_mc, rows_per_gate=bsz * n_ctx, name="out_proj_ctx")

        h = _rmsnorm(xl, nfw, shift=shift_f, scale=scale_f, rows_per_vec=seq,
                     out_dtype=bf16 if layer % 2 == 0 else jnp.float32)
        if not last:
            hc = _rmsnorm(xc, nfw, shift=shift_fc, scale=scale_fc, rows_per_vec=bsz * n_ctx, out_dtype=bf16)
        i = layer // 2
        if layer % 2 == 0:
            wg, wu, wd = dense_w_gate[i].astype(bf16), dense_w_up[i].astype(bf16), dense_w_down[i].astype(bf16)
            t = _matmul(h, [wg, wu], tm=2048, tn=256, tk=d, out_dtype=bf16, name="ffn_gate_up")
            xl = _matmul(t, [wd], tm=512, tn=512, tk=D_FF, out_dtype=jnp.float32, n_major=True,
                         res=xl, gate=gate_f, rows_per_gate=seq, name="ffn_down")
            if not last:
                tc = _matmul(hc, [wg, wu], tm=512, tn=256, tk=d, out_dtype=bf16, name="ffn_gate_up_ctx")
                xc = _matmul(tc, [wd], tm=512, tn=512, tk=D_FF, out_dtype=jnp.float32, n_major=True,
                             res=xc, gate=gate_fc, rows_per_gate=bsz * n_ctx, name="ffn_down_ctx")
        else:
            assert last, "expert layers that still carry a context stream are not supported"
            router = jnp.pad(moe_router[i], ((0, 0), (0, LANE - N_EXPERTS)))
            wg, wu, wd = moe_w_gate[i].astype(bf16), moe_w_up[i].astype(bf16), moe_w_down[i].astype(bf16)
            xl = _moe(h, router, wg, wu, wd, xl, gate_f, seq)

    out = _rmsnorm(xl, final_norm_w[None, :], out_dtype=jnp.float32)
    return out.reshape(bsz, seq, d)
```

```python
import functools
import math

import jax
import jax.numpy as jnp
import numpy as np
from jax import lax
from jax.experimental import pallas as pl
from jax.experimental.pallas import tpu as pltpu

D_MODEL = 4096
DEPTH = 2
GRID_W = 64

RET_WIDTH = D_MODEL // 4
RET_HEAD_DIM = 128
RET_HEADS = RET_WIDTH // RET_HEAD_DIM
SSD_WIDTH = D_MODEL // 2
SSD_HEAD_DIM = 64
SSD_HEADS = SSD_WIDTH // SSD_HEAD_DIM
SSD_GROUPS = 8
SSD_HEADS_PER_GROUP = SSD_HEADS // SSD_GROUPS
SSD_STATE = 128
SSD_CONV = 5
SSD_CONV_DIM = SSD_WIDTH + 2 * SSD_GROUPS * SSD_STATE
DIFF_WIDTH = D_MODEL // 4
DIFF_V_DIM = 128
DIFF_HEADS = DIFF_WIDTH // DIFF_V_DIM
DIFF_QK_DIM = DIFF_V_DIM // 2
MIX_WIDTH = RET_WIDTH + SSD_WIDTH + DIFF_WIDTH

IN_SIZES = (RET_WIDTH, RET_WIDTH, RET_WIDTH, RET_WIDTH,
            SSD_WIDTH, SSD_CONV_DIM, 2 * SSD_HEADS,
            2 * DIFF_HEADS * DIFF_QK_DIM, 2 * DIFF_HEADS * DIFF_QK_DIM, DIFF_WIDTH)
IN_WIDTH = sum(IN_SIZES)
IN_OFFS = tuple(int(s) for s in np.cumsum((0,) + IN_SIZES))
DT_WIDTH = 2 * SSD_HEADS
LANE = 128
DT_PAD = LANE
MAIN_WIDTH = IN_WIDTH - DT_WIDTH

COL_RET_Q, COL_RET_K, COL_RET_V, COL_RET_G = 0, 8, 16, 24
COL_SSD_Z = 32
COL_SSD_X, COL_SSD_B, COL_SSD_C = 48, 64, 72
COL_DIFF_Q, COL_DIFF_K, COL_DIFF_V = 80, 88, 96

KV_CHUNK = 256
SCAN_T = 256
GROUP_W = SSD_HEADS_PER_GROUP * SSD_HEAD_DIM
N_DIR_HEADS = 2 * SSD_HEADS_PER_GROUP

ROPE_BASE = 10000.0
LOG2_E = math.log2(math.e)
D_FF = 11008
N_EXPERTS = 8
TOP_K = 2
MOE_BLOCK = 512
NORM_EPS = 1e-6

VMEM_LIMIT = 56 * 1024 * 1024


def _cparams(sem):
    return pltpu.CompilerParams(dimension_semantics=sem, vmem_limit_bytes=VMEM_LIMIT)


def _silu(x):
    return x * jax.nn.sigmoid(x)


def _softplus(x):
    return jnp.maximum(x, 0.0) + jnp.log(1.0 + jnp.exp(-jnp.abs(x)))


def _f32_dot(a, b):
    return jnp.dot(a, b, preferred_element_type=jnp.float32, precision=lax.Precision.HIGHEST)


def _bf16_dot(a, b):
    return jnp.dot(a.astype(jnp.bfloat16), b.astype(jnp.bfloat16), preferred_element_type=jnp.float32)


def _mm_body(*refs, nk, n_rhs, has_res, grouped):
    refs = list(refs)
    if grouped:
        refs.pop(0)
    a_ref = refs.pop(0)
    b_refs = [refs.pop(0) for _ in range(n_rhs)]
    if has_res:
        res_ref = refs.pop(0)
        gate_ref = refs.pop(0)
    o_ref = refs.pop(0)
    acc_refs = refs

    def epilogue(accs):
        if n_rhs == 2:
            g, u = accs
            val = _silu(g) * u
        else:
            val = accs[0]
        if has_res:
            val = res_ref[...] + gate_ref[...] * val
        o_ref[...] = val.astype(o_ref.dtype)

    a = a_ref[...]
    if b_refs[0].dtype == jnp.float32:
        precision = lax.Precision.HIGHEST
    else:
        precision = None
        a = a.astype(b_refs[0].dtype)
    prods = [jnp.dot(a, b[...], preferred_element_type=jnp.float32, precision=precision) for b in b_refs]
    if nk == 1:
        epilogue(prods)
        return
    k = pl.program_id(2)

    @pl.when(k == 0)
    def _():
        for acc, p in zip(acc_refs, prods):
            acc[...] = p

    @pl.when(k > 0)
    def _():
        for acc, p in zip(acc_refs, prods):
            acc[...] += p

    @pl.when(k == nk - 1)
    def _():
        epilogue([acc[...] for acc in acc_refs])


def _matmul(a, bs, *, tm, tn, tk, out_dtype, res=None, gate=None, rows_per_gate=None,
            block_expert=None, n_major=False, name="matmul"):
    m, kdim = a.shape
    n = bs[0].shape[-1]
    nk = kdim // tk
    grouped = block_expert is not None
    n_rhs = len(bs)
    has_res = res is not None
    assert m % tm == 0 and n % tn == 0 and kdim % tk == 0

    if n_major:
        grid = (n // tn, m // tm, nk)
        ij = lambda g0, g1: (g1, g0)
    else:
        grid = (m // tm, n // tn, nk)
        ij = lambda g0, g1: (g0, g1)

    def a_map(g0, g1, k, *_):
        i, _j = ij(g0, g1)
        return (i, k)

    def b_map(g0, g1, k, *pref):
        i, j = ij(g0, g1)
        if grouped:
            return (pref[0][i], k, j)
        return (k, j)

    def o_map(g0, g1, k, *_):
        return ij(g0, g1)

    def gate_map(g0, g1, k, *_):
        i, j = ij(g0, g1)
        return ((i * tm) // rows_per_gate, 0, j)

    in_specs = [pl.BlockSpec((tm, tk), a_map)]
    b_block = (None, tk, tn) if grouped else (tk, tn)
    in_specs += [pl.BlockSpec(b_block, b_map) for _ in bs]
    operands = [a] + list(bs)
    if has_res:
        in_specs += [pl.BlockSpec((tm, tn), o_map), pl.BlockSpec((None, 1, tn), gate_map)]
        operands += [res, gate]
    scratch = [pltpu.VMEM((tm, tn), jnp.float32) for _ in bs] if nk > 1 else []
    body = functools.partial(_mm_body, nk=nk, n_rhs=n_rhs, has_res=has_res, grouped=grouped)
    grid_spec = pltpu.PrefetchScalarGridSpec(
        num_scalar_prefetch=1 if grouped else 0, grid=grid, in_specs=in_specs,
        out_specs=pl.BlockSpec((tm, tn), o_map), scratch_shapes=scratch)
    call = pl.pallas_call(
        body, grid_spec=grid_spec, out_shape=jax.ShapeDtypeStruct((m, n), out_dtype),
        compiler_params=_cparams(("parallel", "parallel", "arbitrary")), name=name)
    if grouped:
        return call(block_expert, *operands)
    return call(*operands)


def _ada_body(c_ref, w_ref, b_ref, o_ref):
    act = _silu(c_ref[...]).astype(jnp.bfloat16)
    o_ref[...] = jnp.dot(act, w_ref[...].astype(jnp.bfloat16),
                         preferred_element_type=jnp.float32) + b_ref[...]


def _ada_modulation(cond, w, b, layer):
    rows, d = cond.shape
    n = w.shape[2]
    tn = 512
    return pl.pallas_call(
        _ada_body, grid=(n // tn,),
        in_specs=[pl.BlockSpec((rows, d), lambda j: (0, 0)),
                  pl.BlockSpec((None, d, tn), lambda j: (layer, 0, j)),
                  pl.BlockSpec((None, 1, tn), lambda j: (layer, 0, j))],
        out_specs=pl.BlockSpec((rows, tn), lambda j: (0, j)),
        out_shape=jax.ShapeDtypeStruct((rows, n), jnp.float32),
        compiler_params=_cparams(("parallel",)), name="ada_modulation")(cond, w, b)


def _norm_body(*refs, modulated):
    if modulated:
        x_ref, w_ref, shift_ref, scale_ref, o_ref = refs
    else:
        x_ref, w_ref, o_ref = refs
    x = x_ref[...]
    y = x * lax.rsqrt(jnp.mean(x * x, axis=-1, keepdims=True) + NORM_EPS) * w_ref[...]
    if modulated:
        y = y * (1.0 + scale_ref[...]) + shift_ref[...]
    o_ref[...] = y.astype(o_ref.dtype)


def _rmsnorm(x, w, *, shift=None, scale=None, rows_per_vec=None, out_dtype, tm=256):
    m, d = x.shape
    modulated = shift is not None
    in_specs = [pl.BlockSpec((tm, d), lambda i: (i, 0)), pl.BlockSpec((1, d), lambda i: (0, 0))]
    operands = [x, w]
    if modulated:
        vec = pl.BlockSpec((None, 1, d), lambda i: ((i * tm) // rows_per_vec, 0, 0))
        in_specs += [vec, vec]
        operands += [shift, scale]
    return pl.pallas_call(
        functools.partial(_norm_body, modulated=modulated), grid=(m // tm,),
        in_specs=in_specs, out_specs=pl.BlockSpec((tm, d), lambda i: (i, 0)),
        out_shape=jax.ShapeDtypeStruct((m, d), out_dtype),
        compiler_params=_cparams(("parallel",)), name="rmsnorm")(*operands)


def _rope_tables(seq, head_dim, reps):
    rows = seq // GRID_W
    r = jnp.broadcast_to(jnp.arange(rows, dtype=jnp.float32)[:, None], (rows, GRID_W)).reshape(seq)
    col = jnp.broadcast_to(jnp.arange(GRID_W, dtype=jnp.float32)[None, :], (rows, GRID_W)).reshape(seq)
    n_freq = head_dim // 4
    inv_freq = ROPE_BASE ** (-jnp.arange(n_freq, dtype=jnp.float32) / n_freq)
    ang_r = r[:, None] * inv_freq
    ang_c = col[:, None] * inv_freq
    ang = jnp.concatenate([ang_r, ang_r, ang_c, ang_c], axis=-1)
    sign = jnp.concatenate([-jnp.ones(n_freq), jnp.ones(n_freq), -jnp.ones(n_freq), jnp.ones(n_freq)])
    return jnp.tile(jnp.cos(ang), (1, reps)), jnp.tile(jnp.sin(ang) * sign, (1, reps))


def _rope(t, cos, sin_signed, quarter):
    lane = lax.broadcasted_iota(jnp.int32, t.shape, 1)
    width = t.shape[1]
    partner = jnp.where(lane % (2 * quarter) < quarter,
                        pltpu.roll(t, width - quarter, 1), pltpu.roll(t, quarter, 1))
    return t * cos + partner * sin_signed


def _ssd_conv_body(x_ref, w_ref, b_ref, o_ref, *, seq, transpose, chunk):
    w = w_ref[...]
    bias = b_ref[...]
    half = SSD_CONV // 2
    for s in range(0, seq, chunk):
        lo, hi = max(s - 8, 0), min(s + chunk + 8, seq)
        xa = x_ref[lo:hi, :]
        n = hi - lo
        t_glob = lax.broadcasted_iota(jnp.int32, (n, 1), 0) + lo
        acc = xa * w[half:half + 1, :]
        for kk in range(SSD_CONV):
            d = kk - half
            if d == 0:
                continue
            shifted = pltpu.roll(xa, (-d) % n, 0)
            valid = jnp.logical_and(t_glob + d >= 0, t_glob + d < seq)
            acc = acc + jnp.where(valid, shifted, 0.0) * w[kk:kk + 1, :]
        y = _silu(acc + bias)[s - lo:s - lo + chunk, :]
        if transpose:
            o_ref[:, s:s + chunk] = y.T.astype(o_ref.dtype)
        else:
            o_ref[s:s + chunk, :] = y.astype(o_ref.dtype)


def _ssd_conv(proj, conv_w, conv_b, *, bsz, seq, col0, ncols, conv_col0, transpose, out_dtype):
    chunk = min(512, seq)
    if transpose:
        out_shape = jax.ShapeDtypeStruct((bsz, ncols * LANE, seq), out_dtype)
        out_spec = pl.BlockSpec((None, LANE, seq), lambda b, j: (b, j, 0))
    else:
        out_shape = jax.ShapeDtypeStruct((bsz * seq, ncols * LANE), out_dtype)
        out_spec = pl.BlockSpec((seq, LANE), lambda b, j: (b, j))
    return pl.pallas_call(
        functools.partial(_ssd_conv_body, seq=seq, transpose=transpose, chunk=chunk),
        grid=(bsz, ncols),
        in_specs=[pl.BlockSpec((seq, LANE), lambda b, j: (b, col0 + j)),
                  pl.BlockSpec((SSD_CONV, LANE), lambda b, j: (0, conv_col0 + j)),
                  pl.BlockSpec((1, LANE), lambda b, j: (0, conv_col0 + j))],
        out_specs=out_spec, out_shape=out_shape,
        compiler_params=_cparams(("parallel", "parallel")), name="ssd_conv")(proj, conv_w, conv_b)


def _upper_ones(n):
    r = lax.broadcasted_iota(jnp.int32, (n, n), 0)
    c = lax.broadcasted_iota(jnp.int32, (n, n), 1)
    return (r <= c).astype(jnp.float32)


def _head_lane_select(vals, width, shape):
    lane = lax.broadcasted_iota(jnp.int32, shape, len(shape) - 1)
    out = jnp.broadcast_to(vals[-1], shape)
    for r in range(len(vals) - 2, -1, -1):
        out = jnp.where(lane < (r + 1) * width, vals[r], out)
    return out


def _ssd_row_decay(dt_row_ref, bias_row_ref, aneg_row_ref):
    dt_r = _softplus(dt_row_ref[...] + bias_row_ref[...])
    a_r = dt_r * aneg_row_ref[...]
    cs_r = _f32_dot(a_r, _upper_ones(a_r.shape[1]))
    return dt_r, a_r, cs_r


def _ssd_bwd_state_body(bt_ref, x_ref, dt_row_ref, bias_row_ref, aneg_row_ref, s0_ref,
                        s_after_ref, s_fin_ref, s_scr, *, nc):
    k = pl.program_id(2)
    nh = SSD_HEADS_PER_GROUP

    @pl.when(k == 0)
    def _():
        s_scr[...] = s0_ref[...]

    s_after_ref[...] = s_scr[...]
    dt_r, a_r, cs_r = _ssd_row_decay(dt_row_ref, bias_row_ref, aneg_row_ref)
    t = a_r.shape[1]
    ecs_r = cs_r - a_r
    bt = bt_ref[...].astype(jnp.float32)
    xb = x_ref[...].astype(jnp.bfloat16)
    upd = []
    tot = []
    for r in range(nh):
        w_row = jnp.exp(ecs_r[nh + r:nh + r + 1, :]) * dt_r[nh + r:nh + r + 1, :]
        upd.append(_bf16_dot(bt * w_row, xb))
        tot.append(jnp.exp(cs_r[nh + r:nh + r + 1, t - 1:t]))
    shape = s_scr.shape
    s_new = _head_lane_select(tot, SSD_HEAD_DIM, shape) * s_scr[...] + _head_lane_select(upd, SSD_HEAD_DIM, shape)
    s_scr[...] = s_new

    @pl.when(k == nc - 1)
    def _():
        s_fin_ref[...] = s_new


def _ssd_bwd_state(bt, xs, dt_row, bias_row, aneg_row, s0, *, bsz, seq):
    nc = seq // SCAN_T
    g = SSD_GROUPS
    rev = lambda k: nc - 1 - k
    return pl.pallas_call(
        functools.partial(_ssd_bwd_state_body, nc=nc),
        grid=(bsz, g, nc),
        in_specs=[pl.BlockSpec((None, SSD_STATE, SCAN_T), lambda b, gi, k: (b, gi, rev(k))),
                  pl.BlockSpec((SCAN_T, GROUP_W), lambda b, gi, k: (b * nc + rev(k), gi)),
                  pl.BlockSpec((None, None, N_DIR_HEADS, SCAN_T), lambda b, gi, k: (b, gi, 0, rev(k))),
                  pl.BlockSpec((None, N_DIR_HEADS, 1), lambda b, gi, k: (gi, 0, 0)),
                  pl.BlockSpec((None, N_DIR_HEADS, 1), lambda b, gi, k: (gi, 0, 0)),
                  pl.BlockSpec((None, None, SSD_STATE, GROUP_W), lambda b, gi, k: (b, gi, 0, 0))],
        out_specs=[pl.BlockSpec((None, None, None, SSD_STATE, GROUP_W), lambda b, gi, k: (b, gi, rev(k), 0, 0)),
                   pl.BlockSpec((None, None, SSD_STATE, GROUP_W), lambda b, gi, k: (b, gi, 0, 0))],
        out_shape=[jax.ShapeDtypeStruct((bsz, g, nc, SSD_STATE, GROUP_W), jnp.float32),
                   jax.ShapeDtypeStruct((bsz, g, SSD_STATE, GROUP_W), jnp.float32)],
        scratch_shapes=[pltpu.VMEM((SSD_STATE, GROUP_W), jnp.float32)],
        compiler_params=_cparams(("parallel", "parallel", "arbitrary")),
        name="ssd_bwd_state")(bt, xs, dt_row, bias_row, aneg_row, s0)


def _ssd_fwd_body(c_ref, bt_ref, x_ref, z_ref, dt_row_ref, bias_row_ref, aneg_row_ref, dskip_ref, nw_ref,
                  s0_ref, s_after_ref, o_ref, s_fin_ref, s_scr, *, nc):
    k = pl.program_id(2)
    nh = SSD_HEADS_PER_GROUP

    @pl.when(k == 0)
    def _():
        s_scr[...] = s0_ref[...]

    dt_r, a_r, cs_r = _ssd_row_decay(dt_row_ref, bias_row_ref, aneg_row_ref)
    t = a_r.shape[1]
    ecs_r = cs_r - a_r
    n_vec = 2 * N_DIR_HEADS
    cols = jnp.concatenate([cs_r, ecs_r, jnp.zeros((LANE - n_vec, t), jnp.float32)], axis=0).T
    cs_c = cols[:, :N_DIR_HEADS]
    ecs_c = cols[:, N_DIR_HEADS:n_vec]

    cq = c_ref[...]
    bt = bt_ref[...]
    x = x_ref[...]
    xb = x.astype(jnp.bfloat16)
    scores = jnp.dot(cq, bt, preferred_element_type=jnp.float32)
    row = lax.broadcasted_iota(jnp.int32, (t, t), 0)
    col = lax.broadcasted_iota(jnp.int32, (t, t), 1)
    neg_inf = jnp.float32(-jnp.inf)
    btf = bt.astype(jnp.float32)

    intra, e_f, e_b, upd, tot = [], [], [], [], []
    for r in range(nh):
        fr, br = r, nh + r
        seg_f = jnp.where(row >= col, cs_c[:, fr:fr + 1] - cs_r[fr:fr + 1, :], neg_inf)
        seg_b = jnp.where(col >= row, ecs_r[br:br + 1, :] - ecs_c[:, br:br + 1], neg_inf)
        wgt = jnp.exp(seg_f) * dt_r[fr:fr + 1, :] + jnp.exp(seg_b) * dt_r[br:br + 1, :]
        intra.append(_bf16_dot(scores * wgt, xb))
        e_f.append(jnp.exp(cs_c[:, fr:fr + 1]))
        e_b.append(jnp.exp(cs_r[br:br + 1, t - 1:t] - ecs_c[:, br:br + 1]))
        tot_f = cs_r[fr:fr + 1, t - 1:t]
        w_row = jnp.exp(tot_f - cs_r[fr:fr + 1, :]) * dt_r[fr:fr + 1, :]
        upd.append(_bf16_dot(btf * w_row, xb))
        tot.append(jnp.exp(tot_f))

    shape = x.shape
    s_f = s_scr[...]
    y = _head_lane_select(intra, SSD_HEAD_DIM, shape)
    y = y + _head_lane_select(e_f, SSD_HEAD_DIM, shape) * _bf16_dot(cq, s_f)
    y = y + _head_lane_select(e_b, SSD_HEAD_DIM, shape) * _bf16_dot(cq, s_after_ref[...])
    y = (y + dskip_ref[...] * x) * _silu(z_ref[...])
    y = y * lax.rsqrt(jnp.mean(y * y, axis=-1, keepdims=True) + NORM_EPS) * nw_ref[...]
    o_ref[...] = y.astype(o_ref.dtype)

    s_new = (_head_lane_select(tot, SSD_HEAD_DIM, s_f.shape) * s_f
             + _head_lane_select(upd, SSD_HEAD_DIM, s_f.shape))
    s_scr[...] = s_new

    @pl.when(k == nc - 1)
    def _():
        s_fin_ref[...] = s_new


def _ssd_fwd(cm, bt, xs, proj, dt_row, bias_row, aneg_row, dskip, norm_w, s0, s_after, *, bsz, seq):
    nc = seq // SCAN_T
    g = SSD_GROUPS
    zcol = COL_SSD_Z * LANE // GROUP_W
    vec_row = pl.BlockSpec((None, N_DIR_HEADS, 1), lambda b, gi, k: (gi, 0, 0))
    return pl.pallas_call(
        functools.partial(_ssd_fwd_body, nc=nc),
        grid=(bsz, g, nc),
        in_specs=[pl.BlockSpec((SCAN_T, SSD_STATE), lambda b, gi, k: (b * nc + k, gi)),
                  pl.BlockSpec((None, SSD_STATE, SCAN_T), lambda b, gi, k: (b, gi, k)),
                  pl.BlockSpec((SCAN_T, GROUP_W), lambda b, gi, k: (b * nc + k, gi)),
                  pl.BlockSpec((SCAN_T, GROUP_W), lambda b, gi, k: (b * nc + k, zcol + gi)),
                  pl.BlockSpec((None, None, N_DIR_HEADS, SCAN_T), lambda b, gi, k: (b, gi, 0, k)),
                  vec_row, vec_row,
                  pl.BlockSpec((1, GROUP_W), lambda b, gi, k: (0, gi)),
                  pl.BlockSpec((1, GROUP_W), lambda b, gi, k: (0, gi)),
                  pl.BlockSpec((None, None, SSD_STATE, GROUP_W), lambda b, gi, k: (b, gi, 0, 0)),
                  pl.BlockSpec((None, None, None, SSD_STATE, GROUP_W), lambda b, gi, k: (b, gi, k, 0, 0))],
        out_specs=[pl.BlockSpec((SCAN_T, GROUP_W), lambda b, gi, k: (b * nc + k, gi)),
                   pl.BlockSpec((None, None, SSD_STATE, GROUP_W), lambda b, gi, k: (b, gi, 0, 0))],
        out_shape=[jax.ShapeDtypeStruct((bsz * seq, SSD_WIDTH), jnp.bfloat16),
                   jax.ShapeDtypeStruct((bsz, g, SSD_STATE, GROUP_W), jnp.float32)],
        scratch_shapes=[pltpu.VMEM((SSD_STATE, GROUP_W), jnp.float32)],
        compiler_params=_cparams(("parallel", "parallel", "arbitrary")), name="ssd_fwd")(
            cm, bt, xs, proj, dt_row, bias_row, aneg_row, dskip, norm_w, s0, s_after)


def _ssd_group(proj, proj_dt, proj_c, proj_c_dt, lw, *, bsz, seq, n_ctx, need_ctx_out):
    g, nh = SSD_GROUPS, SSD_HEADS_PER_GROUP
    conv_w = lw['ssd_conv_w']
    conv_b = lw['ssd_conv_b'][None, :]

    def per_group(v):
        return v.astype(jnp.float32).reshape(2, g, nh).transpose(1, 0, 2).reshape(g, 2 * nh)

    bias = per_group(lw['ssd_dt_bias'])
    aneg = per_group(-jnp.exp(lw['ssd_a_log'].astype(jnp.float32)))
    bias_row, aneg_row = bias[:, :, None], aneg[:, :, None]
    dskip = jnp.repeat(lw['ssd_d'].astype(jnp.float32), SSD_HEAD_DIM)[None, :]
    norm_w = lw['ssd_norm_w'][None, :]

    def prepare(p, p_dt, n):
        conv = functools.partial(_ssd_conv, p, conv_w, conv_b, bsz=bsz, seq=n)
        xs = conv(col0=COL_SSD_X, ncols=SSD_WIDTH // LANE, conv_col0=0, transpose=False, out_dtype=jnp.float32)
        bt = conv(col0=COL_SSD_B, ncols=g, conv_col0=SSD_WIDTH // LANE, transpose=True, out_dtype=jnp.bfloat16)
        cm = conv(col0=COL_SSD_C, ncols=g, conv_col0=SSD_WIDTH // LANE + g, transpose=False,
                  out_dtype=jnp.bfloat16)
        dt_row = p_dt[:, :DT_WIDTH].reshape(bsz, n, 2, g, nh).transpose(0, 3, 2, 4, 1).reshape(bsz, g, 2 * nh, n)
        return xs, bt, cm, dt_row

    zeros = jnp.zeros((bsz, g, SSD_STATE, GROUP_W), jnp.float32)
    xs_c, bt_c, cm_c, dtr_c = prepare(proj_c, proj_c_dt, n_ctx)
    xs, bt, cm, dt_row = prepare(proj, proj_dt, seq)
    sa_c, sb0 = _ssd_bwd_state(bt_c, xs_c, dtr_c, bias_row, aneg_row, zeros, bsz=bsz, seq=n_ctx)
    fwd = functools.partial(_ssd_fwd, bias_row=bias_row, aneg_row=aneg_row, dskip=dskip, norm_w=norm_w, bsz=bsz)
    y_c, sf0 = fwd(cm_c, bt_c, xs_c, proj_c, dtr_c, s0=zeros, s_after=sa_c, seq=n_ctx)
    sa, _ = _ssd_bwd_state(bt, xs, dt_row, bias_row, aneg_row, sb0, bsz=bsz, seq=seq)
    y, _ = fwd(cm, bt, xs, proj, dt_row, s0=sf0, s_after=sa, seq=seq)
    return y, (y_c if need_ctx_out else None)


def _ret_prep_body(q_ref, k_ref, cos_ref, sin_ref, qo_ref, kto_ref, *, use_rope):
    q = q_ref[...]
    kk = k_ref[...]
    if use_rope:
        q = _rope(q, cos_ref[...], sin_ref[...], RET_HEAD_DIM // 4)
        kk = _rope(kk, cos_ref[...], sin_ref[...], RET_HEAD_DIM // 4)
    kk = kk * (RET_HEAD_DIM ** -0.5)
    qo_ref[...] = q.astype(qo_ref.dtype)
    kto_ref[...] = kk.T.astype(kto_ref.dtype)


def _ret_prep(proj, cos, sin, *, bsz, seq, use_rope):
    tr = min(512, seq)
    nr = seq // tr
    return pl.pallas_call(
        functools.partial(_ret_prep_body, use_rope=use_rope),
        grid=(bsz, RET_HEADS, nr),
        in_specs=[pl.BlockSpec((tr, LANE), lambda b, h, i: (b * nr + i, COL_RET_Q + h)),
                  pl.BlockSpec((tr, LANE), lambda b, h, i: (b * nr + i, COL_RET_K + h)),
                  pl.BlockSpec((tr, LANE), lambda b, h, i: (i, 0)),
                  pl.BlockSpec((tr, LANE), lambda b, h, i: (i, 0))],
        out_specs=[pl.BlockSpec((tr, LANE), lambda b, h, i: (b * nr + i, h)),
                   pl.BlockSpec((None, LANE, tr), lambda b, h, i: (b, h, i))],
        out_shape=[jax.ShapeDtypeStruct((bsz * seq, RET_WIDTH), jnp.bfloat16),
                   jax.ShapeDtypeStruct((bsz, RET_WIDTH, seq), jnp.bfloat16)],
        compiler_params=_cparams(("parallel", "parallel", "parallel")), name="ret_prep")(proj, proj, cos, sin)


def _ret_bwd_state_body(lg_ref, kt_ref, v_ref, s0_ref, s_after_ref, s_fin_ref, s_scr, *, nc):
    h = pl.program_id(1)
    k = pl.program_id(2)

    @pl.when(k == 0)
    def _():
        s_scr[...] = s0_ref[...]

    s_after_ref[...] = s_scr[...]
    lgb = jnp.full((1, 1), lg_ref[1, h], jnp.float32)
    t = kt_ref.shape[1]
    j = lax.broadcasted_iota(jnp.int32, (1, t), 1).astype(jnp.float32)
    kt = kt_ref[...].astype(jnp.float32) * jnp.exp(j * lgb)
    s_new = jnp.exp(t * lgb) * s_scr[...] + _bf16_dot(kt, v_ref[...])
    s_scr[...] = s_new

    @pl.when(k == nc - 1)
    def _():
        s_fin_ref[...] = s_new


def _ret_bwd_state(lg, kt, proj, s0, *, bsz, seq):
    nc = seq // SCAN_T
    rev = lambda k: nc - 1 - k
    grid_spec = pltpu.PrefetchScalarGridSpec(
        num_scalar_prefetch=0, grid=(bsz, RET_HEADS, nc),
        in_specs=[pl.BlockSpec(memory_space=pltpu.SMEM),
                  pl.BlockSpec((None, LANE, SCAN_T), lambda b, h, k: (b, h, rev(k))),
                  pl.BlockSpec((SCAN_T, LANE), lambda b, h, k: (b * nc + rev(k), COL_RET_V + h)),
                  pl.BlockSpec((None, None, LANE, LANE), lambda b, h, k: (b, h, 0, 0))],
        out_specs=[pl.BlockSpec((None, None, None, LANE, LANE), lambda b, h, k: (b, h, rev(k), 0, 0)),
                   pl.BlockSpec((None, None, LANE, LANE), lambda b, h, k: (b, h, 0, 0))],
        scratch_shapes=[pltpu.VMEM((LANE, LANE), jnp.float32)])
    return pl.pallas_call(
        functools.partial(_ret_bwd_state_body, nc=nc), grid_spec=grid_spec,
        out_shape=[jax.ShapeDtypeStruct((bsz, RET_HEADS, nc, LANE, LANE), jnp.float32),
                   jax.ShapeDtypeStruct((bsz, RET_HEADS, LANE, LANE), jnp.float32)],
        compiler_params=_cparams(("parallel", "parallel", "arbitrary")), name="ret_bwd_state")(lg, kt, proj, s0)


def _ret_fwd_body(lg_ref, q_ref, kt_ref, v_ref, g_ref, nw_ref, s0_ref, s_after_ref,
                  o_ref, s_fin_ref, s_scr, d_scr, *, nc):
    h = pl.program_id(1)
    k = pl.program_id(2)
    lgf = jnp.full((1, 1), lg_ref[0, h], jnp.float32)
    lgb = jnp.full((1, 1), lg_ref[1, h], jnp.float32)
    t = q_ref.shape[0]

    @pl.when(k == 0)
    def _():
        s_scr[...] = s0_ref[...]
        row = lax.broadcasted_iota(jnp.int32, (t, t), 0)
        col = lax.broadcasted_iota(jnp.int32, (t, t), 1)
        dist = (row - col).astype(jnp.float32)
        neg_inf = jnp.float32(-jnp.inf)
        d_scr[...] = (jnp.exp(jnp.where(row >= col, dist * lgf, neg_inf))
                      + jnp.exp(jnp.where(col >= row, -dist * lgb, neg_inf)))

    q = q_ref[...]
    kt = kt_ref[...]
    v = v_ref[...].astype(jnp.bfloat16)
    scores = jnp.dot(q, kt, preferred_element_type=jnp.float32)
    i = lax.broadcasted_iota(jnp.int32, (t, 1), 0).astype(jnp.float32)
    j = lax.broadcasted_iota(jnp.int32, (1, t), 1).astype(jnp.float32)
    s_f = s_scr[...]
    y = _bf16_dot(scores * d_scr[...], v)
    y = y + jnp.exp((i + 1.0) * lgf) * _bf16_dot(q, s_f)
    y = y + jnp.exp((t - i) * lgb) * _bf16_dot(q, s_after_ref[...])
    y = y * lax.rsqrt(jnp.mean(y * y, axis=-1, keepdims=True) + NORM_EPS) * nw_ref[...]
    o_ref[...] = (_silu(g_ref[...]) * y).astype(o_ref.dtype)

    ktw = kt.astype(jnp.float32) * jnp.exp((t - 1.0 - j) * lgf)
    s_new = jnp.exp(t * lgf) * s_f + _bf16_dot(ktw, v)
    s_scr[...] = s_new

    @pl.when(k == nc - 1)
    def _():
        s_fin_ref[...] = s_new


def _ret_fwd(lg, q, kt, proj, norm_w, s0, s_after, *, bsz, seq):
    nc = seq // SCAN_T
    grid_spec = pltpu.PrefetchScalarGridSpec(
        num_scalar_prefetch=0, grid=(bsz, RET_HEADS, nc),
        in_specs=[pl.BlockSpec(memory_space=pltpu.SMEM),
                  pl.BlockSpec((SCAN_T, LANE), lambda b, h, k: (b * nc + k, h)),
                  pl.BlockSpec((None, LANE, SCAN_T), lambda b, h, k: (b, h, k)),
                  pl.BlockSpec((SCAN_T, LANE), lambda b, h, k: (b * nc + k, COL_RET_V + h)),
                  pl.BlockSpec((SCAN_T, LANE), lambda b, h, k: (b * nc + k, COL_RET_G + h)),
                  pl.BlockSpec((1, LANE), lambda b, h, k: (0, h)),
                  pl.BlockSpec((None, None, LANE, LANE), lambda b, h, k: (b, h, 0, 0)),
                  pl.BlockSpec((None, None, None, LANE, LANE), lambda b, h, k: (b, h, k, 0, 0))],
        out_specs=[pl.BlockSpec((SCAN_T, LANE), lambda b, h, k: (b * nc + k, h)),
                   pl.BlockSpec((None, None, LANE, LANE), lambda b, h, k: (b, h, 0, 0))],
        scratch_shapes=[pltpu.VMEM((LANE, LANE), jnp.float32), pltpu.VMEM((SCAN_T, SCAN_T), jnp.float32)])
    return pl.pallas_call(
        functools.partial(_ret_fwd_body, nc=nc), grid_spec=grid_spec,
        out_shape=[jax.ShapeDtypeStruct((bsz * seq, RET_WIDTH), jnp.bfloat16),
                   jax.ShapeDtypeStruct((bsz, RET_HEADS, LANE, LANE), jnp.float32)],
        compiler_params=_cparams(("parallel", "parallel", "arbitrary")), name="ret_fwd")(
            lg, q, kt, proj, proj, norm_w, s0, s_after)


def _ret_group(proj, proj_c, lw, rope, *, bsz, seq, n_ctx, need_ctx_out):
    cos, sin = rope
    lg = jax.nn.log_sigmoid(lw['ret_decay_logit'].astype(jnp.float32))
    norm_w = lw['ret_norm_w'][None, :]
    zeros = jnp.zeros((bsz, RET_HEADS, LANE, LANE), jnp.float32)
    q_c, kt_c = _ret_prep(proj_c, cos, sin, bsz=bsz, seq=n_ctx, use_rope=False)
    q, kt = _ret_prep(proj, cos, sin, bsz=bsz, seq=seq, use_rope=True)
    sa_c, sb0 = _ret_bwd_state(lg, kt_c, proj_c, zeros, bsz=bsz, seq=n_ctx)
    y_c, sf0 = _ret_fwd(lg, q_c, kt_c, proj_c, norm_w, zeros, sa_c, bsz=bsz, seq=n_ctx)
    sa, _ = _ret_bwd_state(lg, kt, proj, sb0, bsz=bsz, seq=seq)
    y, _ = _ret_fwd(lg, q, kt, proj, norm_w, sf0, sa, bsz=bsz, seq=seq)
    return y, (y_c if need_ctx_out else None)


def _diff_prep_body(q_ref, k_ref, v_ref, cos_ref, sin_ref, qt1_ref, qt2_ref, ko_ref, vto_ref, *, use_rope):
    q = q_ref[...]
    kk = k_ref[...]
    if use_rope:
        q = _rope(q, cos_ref[...], sin_ref[...], DIFF_QK_DIM // 4)
        kk = _rope(kk, cos_ref[...], sin_ref[...], DIFF_QK_DIM // 4)
    qt = (q * (DIFF_QK_DIM ** -0.5 * LOG2_E)).T
    sub = lax.broadcasted_iota(jnp.int32, qt.shape, 0)
    qt1_ref[...] = jnp.where(sub < DIFF_QK_DIM, qt, 0.0).astype(qt1_ref.dtype)
    qt2_ref[...] = jnp.where(sub >= DIFF_QK_DIM, qt, 0.0).astype(qt2_ref.dtype)
    ko_ref[...] = kk.astype(ko_ref.dtype)
    vto_ref[...] = v_ref[...].T.astype(vto_ref.dtype)


def _diff_prep(proj, cos, sin, *, bsz, seq, use_rope):
    tr = KV_CHUNK
    nr = seq // tr
    t_spec = pl.BlockSpec((None, LANE, tr), lambda b, h, i: (b, h, i))
    t_shape = jax.ShapeDtypeStruct((bsz, DIFF_WIDTH, seq), jnp.bfloat16)
    return pl.pallas_call(
        functools.partial(_diff_prep_body, use_rope=use_rope),
        grid=(bsz, DIFF_HEADS, nr),
        in_specs=[pl.BlockSpec((tr, LANE), lambda b, h, i: (b * nr + i, COL_DIFF_Q + h)),
                  pl.BlockSpec((tr, LANE), lambda b, h, i: (b * nr + i, COL_DIFF_K + h)),
                  pl.BlockSpec((tr, LANE), lambda b, h, i: (b * nr + i, COL_DIFF_V + h)),
                  pl.BlockSpec((tr, LANE), lambda b, h, i: (i, 0)),
                  pl.BlockSpec((tr, LANE), lambda b, h, i: (i, 0))],
        out_specs=[t_spec, t_spec,
                   pl.BlockSpec((None, tr, LANE), lambda b, h, i: (b, i, h)),
                   pl.BlockSpec((None, None, LANE, tr), lambda b, h, i: (b, i, h, 0))],
        out_shape=[t_shape, t_shape,
                   jax.ShapeDtypeStruct((bsz, seq, DIFF_WIDTH), jnp.bfloat16),
                   jax.ShapeDtypeStruct((bsz, nr, DIFF_WIDTH, tr), jnp.bfloat16)],
        compiler_params=_cparams(("parallel", "parallel", "parallel")), name="diff_prep")(
            proj, proj, proj, cos, sin)


def _diff_attn_body(lam_ref, qt1_ref, qt2_ref, k_ref, vt_ref, nw_ref, o_ref, acc1, acc2, s_scr, *,
                    chunks, out_scale):
    tk = chunks * KV_CHUNK
    n_steps = k_ref.shape[0] // tk
    tq = qt1_ref.shape[1]
    qts = (qt1_ref[...], qt2_ref[...])
    accs = (acc1, acc2)
    for acc in accs:
        acc[...] = jnp.zeros_like(acc)

    def scores(step, slot):
        start = pl.multiple_of(step * tk, tk)
        kk = k_ref[pl.ds(start, tk), :]
        for mp in range(2):
            s_scr[slot, mp] = jnp.dot(kk, qts[mp], preferred_element_type=jnp.float32)

    def consume(step, slot, carry):
        new = []
        for mp in range(2):
            m, l = carry[mp]
            s = s_scr[slot, mp]
            m_new = jnp.maximum(m, jnp.max(s, axis=0, keepdims=True))
            alpha = jnp.exp2(m - m_new)
            p = jnp.exp2(s - m_new)
            l_new = alpha * l + jnp.sum(p, axis=0, keepdims=True)
            pb = p.astype(jnp.bfloat16)
            pv = jnp.dot(vt_ref[step * chunks], pb[:KV_CHUNK], preferred_element_type=jnp.float32)
            for c in range(1, chunks):
                pv = pv + jnp.dot(vt_ref[step * chunks + c], pb[c * KV_CHUNK:(c + 1) * KV_CHUNK],
                                  preferred_element_type=jnp.float32)
            accs[mp][...] = alpha * accs[mp][...] + pv
            new.append((m_new, l_new))
        return tuple(new)

    def pair(jj, carry):
        j = 2 * jj
        scores(j + 1, 1)
        carry = consume(j, 0, carry)
        scores(jnp.minimum(j + 2, n_steps - 1), 0)
        return consume(j + 1, 1, carry)

    init = tuple((jnp.full((1, tq), -jnp.inf, jnp.float32), jnp.zeros((1, tq), jnp.float32)) for _ in range(2))
    scores(0, 0)
    carry = lax.fori_loop(0, n_steps // 2, pair, init)
    if n_steps % 2:
        carry = consume(n_steps - 1, 0, carry)
    (m1, l1), (m2, l2) = carry
    o = acc1[...] / l1 - lam_ref[0] * (acc2[...] / l2)
    o = o * lax.rsqrt(jnp.mean(o * o, axis=0, keepdims=True) + NORM_EPS)
    o_ref[...] = (o.T * nw_ref[...] * out_scale).astype(o_ref.dtype)


def _diff_attn(lam, qt1, qt2, k_all, vt_all, norm_w, *, bsz, n_q, lambda_init, tq):
    nq = n_q // tq
    kv_len = k_all.shape[1]
    n_chunks = kv_len // KV_CHUNK
    chunks = 3 if n_chunks % 3 == 0 else 1
    grid_spec = pltpu.PrefetchScalarGridSpec(
        num_scalar_prefetch=0, grid=(bsz, DIFF_HEADS, nq),
        in_specs=[pl.BlockSpec(memory_space=pltpu.SMEM),
                  pl.BlockSpec((None, LANE, tq), lambda b, h, i: (b, h, i)),
                  pl.BlockSpec((None, LANE, tq), lambda b, h, i: (b, h, i)),
                  pl.BlockSpec((None, kv_len, LANE), lambda b, h, i: (b, 0, h)),
                  pl.BlockSpec((None, n_chunks, LANE, KV_CHUNK), lambda b, h, i: (b, 0, h, 0)),
                  pl.BlockSpec((1, LANE), lambda b, h, i: (0, 0))],
        out_specs=pl.BlockSpec((tq, LANE), lambda b, h, i: (b * nq + i, h)),
        scratch_shapes=[pltpu.VMEM((LANE, tq), jnp.float32), pltpu.VMEM((LANE, tq), jnp.float32),
                        pltpu.VMEM((2, 2, chunks * KV_CHUNK, tq), jnp.float32)])
    return pl.pallas_call(
        functools.partial(_diff_attn_body, chunks=chunks, out_scale=1.0 - lambda_init), grid_spec=grid_spec,
        out_shape=jax.ShapeDtypeStruct((bsz * n_q, DIFF_WIDTH), jnp.bfloat16),
        compiler_params=_cparams(("parallel", "parallel", "arbitrary")), name="diff_attn")(
            lam, qt1, qt2, k_all, vt_all, norm_w)


def _diff_group(proj, proj_c, lw, rope, *, bsz, seq, n_ctx, lambda_init, need_ctx_out):
    cos, sin = rope
    lv = lw['diff_lambda'].astype(jnp.float32)
    lam = (jnp.exp(jnp.sum(lv[0] * lv[1])) - jnp.exp(jnp.sum(lv[2] * lv[3])) + lambda_init).reshape(1)
    norm_w = lw['diff_norm_w'][None, :]
    qt1, qt2, k_l, vt_l = _diff_prep(proj, cos, sin, bsz=bsz, seq=seq, use_rope=True)
    qt1c, qt2c, k_c, vt_c = _diff_prep(proj_c, cos, sin, bsz=bsz, seq=n_ctx, use_rope=False)
    k_all = jnp.concatenate([k_l, k_c], axis=1)
    vt_all = jnp.concatenate([vt_l, vt_c], axis=1)
    y = _diff_attn(lam, qt1, qt2, k_all, vt_all, norm_w, bsz=bsz, n_q=seq, lambda_init=lambda_init,
                   tq=min(1024, seq))
    y_c = None
    if need_ctx_out:
        y_c = _diff_attn(lam, qt1c, qt2c, k_c, vt_c, norm_w, bsz=bsz, n_q=n_ctx, lambda_init=lambda_init,
                         tq=n_ctx)
    return y, y_c


def _mixers(proj, proj_dt, proj_c, proj_c_dt, lw, *, bsz, seq, n_ctx, lambda_init, need_ctx_out):
    rope_ret = _rope_tables(seq, RET_HEAD_DIM, 1)
    rope_diff = _rope_tables(seq, DIFF_QK_DIM, 2)
    r_l, r_c = _ret_group(proj, proj_c, lw, rope_ret, bsz=bsz, seq=seq, n_ctx=n_ctx, need_ctx_out=need_ctx_out)
    s_l, s_c = _ssd_group(proj, proj_dt, proj_c, proj_c_dt, lw, bsz=bsz, seq=seq, n_ctx=n_ctx,
                          need_ctx_out=need_ctx_out)
    d_l, d_c = _diff_group(proj, proj_c, lw, rope_diff, bsz=bsz, seq=seq, n_ctx=n_ctx,
                           lambda_init=lambda_init, need_ctx_out=need_ctx_out)
    mix = jnp.concatenate([r_l, s_l, d_l], axis=-1)
    mix_c = jnp.concatenate([r_c, s_c, d_c], axis=-1) if need_ctx_out else None
    return mix, mix_c


def _gather_rows_body(first_ref, nrows_ref, tok_ref, src_hbm, o_ref, sem, *, tm, steps_per_block):
    i = pl.program_id(0)
    blk = i // steps_per_block
    off = (i % steps_per_block) * tm
    first = first_ref[blk] + off
    nrows = nrows_ref[blk] - off
    last = tok_ref.shape[0] - 1

    def row_copy(r):
        tok = jnp.where(r < nrows, tok_ref[jnp.minimum(first + r, last)], 0)
        return pltpu.make_async_copy(src_hbm.at[pl.ds(tok, 1)], o_ref.at[pl.ds(r, 1)], sem)

    def start(r, carry):
        row_copy(r).start()
        return carry

    def wait(r, carry):
        row_copy(r).wait()
        return carry

    lax.fori_loop(0, tm, start, 0)
    lax.fori_loop(0, tm, wait, 0)


def _gather_rows(block_first, block_rows, token_s, src, *, block, tm):
    n_blocks = block_first.shape[0]
    d = src.shape[1]
    steps_per_block = block // tm
    grid_spec = pltpu.PrefetchScalarGridSpec(
        num_scalar_prefetch=3, grid=(n_blocks * steps_per_block,),
        in_specs=[pl.BlockSpec(memory_space=pl.ANY)],
        out_specs=pl.BlockSpec((tm, d), lambda i, *_: (i, 0)),
        scratch_shapes=[pltpu.SemaphoreType.DMA(())])
    return pl.pallas_call(
        functools.partial(_gather_rows_body, tm=tm, steps_per_block=steps_per_block), grid_spec=grid_spec,
        out_shape=jax.ShapeDtypeStruct((n_blocks * block, d), src.dtype),
        compiler_params=_cparams(("arbitrary",)), name="moe_gather_rows")(block_first, block_rows, token_s, src)


def _moe_combine_body(pos_ref, y_hbm, g_ref, res_ref, gate_ref, o_ref, buf, sem, *, tm, n_steps):
    i = pl.program_id(0)

    def row_copy(step, slot, r, kk):
        src_row = pos_ref[(step * tm + r) * TOP_K + kk]
        return pltpu.make_async_copy(y_hbm.at[pl.ds(src_row, 1)], buf.at[slot, kk, pl.ds(r, 1)], sem.at[slot])

    def start_block(step, slot):
        def body(r, carry):
            for kk in range(TOP_K):
                row_copy(step, slot, r, kk).start()
            return carry
        lax.fori_loop(0, tm, body, 0)

    def wait_block(step, slot):
        def body(r, carry):
            for kk in range(TOP_K):
                row_copy(step, slot, r, kk).wait()
            return carry
        lax.fori_loop(0, tm, body, 0)

    slot = i % 2

    @pl.when(i == 0)
    def _():
        start_block(0, 0)

    @pl.when(i + 1 < n_steps)
    def _():
        start_block(i + 1, 1 - slot)

    wait_block(i, slot)
    g = g_ref[...]
    f = g[:, 0:1] * buf[slot, 0]
    for kk in range(1, TOP_K):
        f = f + g[:, kk:kk + 1] * buf[slot, kk]
    o_ref[...] = res_ref[...] + gate_ref[...] * f


def _moe_combine(pos, y_rows, gates, res, gate_vec, rows_per_gate, *, tm=128):
    n_tok, d = res.shape
    n_steps = n_tok // tm
    grid_spec = pltpu.PrefetchScalarGridSpec(
        num_scalar_prefetch=1, grid=(n_steps,),
        in_specs=[pl.BlockSpec(memory_space=pl.ANY),
                  pl.BlockSpec((tm, TOP_K), lambda i, p: (i, 0)),
                  pl.BlockSpec((tm, d), lambda i, p: (i, 0)),
                  pl.BlockSpec((None, 1, d), lambda i, p: ((i * tm) // rows_per_gate, 0, 0))],
        out_specs=pl.BlockSpec((tm, d), lambda i, p: (i, 0)),
        scratch_shapes=[pltpu.VMEM((2, TOP_K, tm, d), jnp.float32), pltpu.SemaphoreType.DMA((2,))])
    return pl.pallas_call(
        functools.partial(_moe_combine_body, tm=tm, n_steps=n_steps), grid_spec=grid_spec,
        out_shape=jax.ShapeDtypeStruct((n_tok, d), jnp.float32),
        compiler_params=_cparams(("arbitrary",)), name="moe_combine")(pos, y_rows, gates, res, gate_vec)


def _moe(h, router_w, w_gate, w_up, w_down, res, gate_vec, rows_per_gate):
    n_tok, d = h.shape
    n_exp = N_EXPERTS
    n_assign = n_tok * TOP_K
    logits = _matmul(h, [router_w], tm=512, tn=LANE, tk=d, out_dtype=jnp.float32, name="moe_router")[:, :n_exp]
    top_logits, top_idx = lax.top_k(logits, TOP_K)
    gates = jax.nn.softmax(top_logits, axis=-1)
    expert = top_idx.reshape(n_assign)
    order = jnp.argsort(expert)
    expert_s = expert[order]
    token_s = order // TOP_K
    counts = jnp.sum(expert[:, None] == jnp.arange(n_exp)[None, :], axis=0, dtype=jnp.int32)
    starts = jnp.cumsum(counts) - counts
    padded = (counts + MOE_BLOCK - 1) // MOE_BLOCK * MOE_BLOCK
    pad_ends = jnp.cumsum(padded)
    pad_starts = pad_ends - padded
    dest = pad_starts[expert_s] + jnp.arange(n_assign) - starts[expert_s]
    n_blocks = -(-n_assign // MOE_BLOCK) + n_exp
    block_expert = jnp.minimum(
        jnp.searchsorted(pad_ends, jnp.arange(n_blocks) * MOE_BLOCK, side='right'), n_exp - 1
    ).astype(jnp.int32)
    block_off = jnp.arange(n_blocks) * MOE_BLOCK - pad_starts[block_expert]
    block_first = (starts[block_expert] + block_off).astype(jnp.int32)
    block_rows = jnp.clip(counts[block_expert] - block_off, 0, MOE_BLOCK).astype(jnp.int32)
    rows = _gather_rows(block_first, block_rows, token_s.astype(jnp.int32), h, block=MOE_BLOCK, tm=MOE_BLOCK // 2)
    t = _matmul(rows, [w_gate, w_up], tm=MOE_BLOCK, tn=512, tk=d, out_dtype=jnp.bfloat16,
                block_expert=block_expert, n_major=True, name="moe_gate_up")
    y_rows = _matmul(t, [w_down], tm=MOE_BLOCK, tn=1024, tk=w_down.shape[1], out_dtype=jnp.float32,
                     block_expert=block_expert, n_major=True, name="moe_down")
    pos = dest[jnp.argsort(order)].astype(jnp.int32)
    return _moe_combine(pos, y_rows, gates, res, gate_vec, rows_per_gate)


def kernel(x, c, ctx, c_ctx, ada_w, ada_b, norm_mix_w, norm_ffn_w, w_in, w_out, ret_decay_logit, ret_norm_w, ssd_conv_w, ssd_conv_b, ssd_dt_bias, ssd_a_log, ssd_d, ssd_norm_w, diff_lambda, diff_norm_w, dense_w_gate, dense_w_up, dense_w_down, moe_router, moe_w_gate, moe_w_up, moe_w_down, final_norm_w):
    bsz, seq, d = x.shape
    n_ctx = ctx.shape[1]
    assert n_ctx % SCAN_T == 0 and n_ctx % KV_CHUNK == 0 and seq % 1024 == 0
    bf16 = jnp.bfloat16

    xl = x.reshape(bsz * seq, d)
    xc = ctx.reshape(bsz * n_ctx, d)
    cond = jnp.concatenate([c, c_ctx[None, :], jnp.zeros((8 - bsz - 1, d), jnp.float32)], axis=0)

    for layer in range(DEPTH):
        last = layer == DEPTH - 1
        lambda_init = 0.8 - 0.6 * math.exp(-0.3 * layer)
        lw = {'ret_decay_logit': ret_decay_logit[layer], 'ret_norm_w': ret_norm_w[layer],
              'ssd_conv_w': ssd_conv_w[layer], 'ssd_conv_b': ssd_conv_b[layer],
              'ssd_dt_bias': ssd_dt_bias[layer], 'ssd_a_log': ssd_a_log[layer], 'ssd_d': ssd_d[layer],
              'ssd_norm_w': ssd_norm_w[layer], 'diff_lambda': diff_lambda[layer],
              'diff_norm_w': diff_norm_w[layer]}

        mod = _ada_modulation(cond, ada_w, ada_b[:, None, :], layer)
        mod_l = mod[:bsz].reshape(bsz, 1, 6, d)
        mod_c = mod[bsz:bsz + 1].reshape(1, 1, 6, d)
        shift_m, scale_m, gate_m, shift_f, scale_f, gate_f = [mod_l[:, :, i] for i in range(6)]
        shift_mc, scale_mc, gate_mc, shift_fc, scale_fc, gate_fc = [mod_c[:, :, i] for i in range(6)]

        w_in_l = w_in[layer]
        w_main = jnp.concatenate([w_in_l[:, :IN_OFFS[6]], w_in_l[:, IN_OFFS[7]:]], axis=1).astype(bf16)
        w_dt = jnp.pad(w_in_l[:, IN_OFFS[6]:IN_OFFS[7]], ((0, 0), (0, DT_PAD - DT_WIDTH))).astype(bf16)
        w_out_l = w_out[layer].astype(bf16)
        nmw = norm_mix_w[layer][None, :]
        nfw = norm_ffn_w[layer][None, :]

        h = _rmsnorm(xl, nmw, shift=shift_m, scale=scale_m, rows_per_vec=seq, out_dtype=bf16)
        hc = _rmsnorm(xc, nmw, shift=shift_mc, scale=scale_mc, rows_per_vec=bsz * n_ctx, out_dtype=bf16)
        proj = _matmul(h, [w_main], tm=1024, tn=1024, tk=d, out_dtype=jnp.float32, name="in_proj")
        proj_dt = _matmul(h, [w_dt], tm=1024, tn=DT_PAD, tk=d, out_dtype=jnp.float32, name="in_proj_dt")
        proj_c = _matmul(hc, [w_main], tm=512, tn=1024, tk=d, out_dtype=jnp.float32, name="in_proj_ctx")
        proj_c_dt = _matmul(hc, [w_dt], tm=512, tn=DT_PAD, tk=d, out_dtype=jnp.float32, name="in_proj_dt_ctx")
        mix, mix_c = _mixers(proj, proj_dt, proj_c, proj_c_dt, lw, bsz=bsz, seq=seq, n_ctx=n_ctx,
                             lambda_init=lambda_init, need_ctx_out=not last)
        xl = _matmul(mix, [w_out_l], tm=512, tn=1024, tk=MIX_WIDTH, out_dtype=jnp.float32,
                     res=xl, gate=gate_m, rows_per_gate=seq, name="out_proj")
        if not last:
            xc = _matmul(mix_c, [w_out_l], tm=512, tn=1024, tk=MIX_WIDTH // 2, out_dtype=jnp.float32,
                         res=xc, gate=gate_mc, rows_per_gate=bsz * n_ctx, name="out_proj_ctx")

        h = _rmsnorm(xl, nfw, shift=shift_f, scale=scale_f, rows_per_vec=seq,
                     out_dtype=bf16 if layer % 2 == 0 else jnp.float32)
        if not last:
            hc = _rmsnorm(xc, nfw, shift=shift_fc, scale=scale_fc, rows_per_vec=bsz * n_ctx, out_dtype=bf16)
        i = layer // 2
        if layer % 2 == 0:
            wg, wu, wd = dense_w_gate[i].astype(bf16), dense_w_up[i].astype(bf16), dense_w_down[i].astype(bf16)
            t = _matmul(h, [wg, wu], tm=2048, tn=256, tk=d, out_dtype=bf16, name="ffn_gate_up")
            xl = _matmul(t, [wd], tm=512, tn=512, tk=D_FF, out_dtype=jnp.float32, n_major=True,
                         res=xl, gate=gate_f, rows_per_gate=seq, name="ffn_down")
            if not last:
                tc = _matmul(hc, [wg, wu], tm=512, tn=256, tk=d, out_dtype=bf16, name="ffn_gate_up_ctx")
                xc = _matmul(tc, [wd], tm=512, tn=512, tk=D_FF, out_dtype=jnp.float32, n_major=True,
                             res=xc, gate=gate_fc, rows_per_gate=bsz * n_ctx, name="ffn_down_ctx")
        else:
            assert last, "expert layers that still carry a context stream are not supported"
            router = jnp.pad(moe_router[i], ((0, 0), (0, LANE - N_EXPERTS)))
            wg, wu, wd = moe_w_gate[i].astype(bf16), moe_w_up[i].astype(bf16), moe_w_down[i].astype(bf16)
            xl = _moe(h, router, wg, wu, wd, xl, gate_f, seq)

    out = _rmsnorm(xl, final_norm_w[None, :], out_dtype=jnp.float32)
    return out.reshape(bsz, seq, d)
```
